```python
import math
import jax, jax.numpy as jnp
from jax import lax
import numpy as np

D_MODEL = 1024
BATCH = 2
SEQ = 8192
DEPTH = 1

PLE_DIM = 256
HEAD_DIM = 128
N_HGRN_HEADS = D_MODEL // (2 * HEAD_DIM)
N_GDN_HEADS = D_MODEL // (2 * HEAD_DIM)
D_HGRN = N_HGRN_HEADS * HEAD_DIM
D_GDN = N_GDN_HEADS * HEAD_DIM
D_MIX = D_HGRN + D_GDN
D_IN = 4 * D_HGRN + 4 * D_GDN + 2 * N_GDN_HEADS
GDN_CONV = 4
FFN_CONV = 3
D_FF = 2816
CHUNK = 64
EPS = 1e-6

kernel_name = "hymba_hgrn2_gdn_convffn_ple"


def rms_norm(x, w):
    xf = x.astype(jnp.float32)
    y = xf * lax.rsqrt(jnp.mean(xf * xf, axis=-1, keepdims=True) + EPS)
    return (y * w.astype(jnp.float32)).astype(x.dtype)


def l2_normalize(x):
    xf = x.astype(jnp.float32)
    return xf * lax.rsqrt(jnp.sum(xf * xf, axis=-1, keepdims=True) + EPS)


def causal_dwconv(x, w):
    k_width, ch = w.shape
    return lax.conv_general_dilated(
        x, w[:, None, :].astype(x.dtype), window_strides=(1,),
        padding=[(k_width - 1, 0)], dimension_numbers=("NWC", "WIO", "NWC"),
        feature_group_count=ch)


def to_heads(t):
    b, s, _ = t.shape
    return t.reshape(b, s, -1, HEAD_DIM).transpose(0, 2, 1, 3)


def from_heads(t):
    b, h, s, d = t.shape
    return t.transpose(0, 2, 1, 3).reshape(b, s, h * d)


def to_chunks(t):
    b, h, s = t.shape[:3]
    return t.reshape((b, h, s // CHUNK, CHUNK) + t.shape[3:])


def hgrn2_chunked(q, k, v, log_f):
    b, h, s, dk = q.shape
    dv = v.shape[-1]
    qc, kc, vc, gc = (jnp.moveaxis(to_chunks(t.astype(jnp.float32)), 2, 0)
                      for t in (q, k, v, log_f))
    bc = jnp.cumsum(gc, axis=-2)
    idx = jnp.arange(CHUNK)
    causal = (idx[:, None] >= idx[None, :])[:, :, None]

    def step(state, inp):
        qi, ki, vi, bi = inp
        diff = bi[..., :, None, :] - bi[..., None, :, :]
        decay = jnp.exp(jnp.where(causal, diff, -jnp.inf))
        scores = jnp.einsum("bhtd,bhsd,bhtsd->bhts", qi, ki, decay)
        o = (jnp.einsum("bhts,bhsv->bhtv", scores, vi)
             + jnp.einsum("bhtd,bhdv->bhtv", qi * jnp.exp(bi), state))
        b_last = bi[..., -1:, :]
        state = (jnp.exp(b_last[..., 0, :])[..., :, None] * state
                 + jnp.einsum("bhsd,bhsv->bhdv", ki * jnp.exp(b_last - bi), vi))
        return state, o

    s0 = jnp.zeros((b, h, dk, dv), jnp.float32)
    _, o = lax.scan(step, s0, (qc, kc, vc, bc))
    return jnp.moveaxis(o, 0, 2).reshape(b, h, s, dv)


def gated_delta_chunked(q, k, v, log_a, beta):
    b, h, s, dk = q.shape
    dv = v.shape[-1]
    qc, kc, vc = (to_chunks(t.astype(jnp.float32)) for t in (q, k, v))
    betac = to_chunks(beta.astype(jnp.float32))
    gcum = jnp.cumsum(to_chunks(log_a.astype(jnp.float32)), axis=-1)
    idx = jnp.arange(CHUNK)
    incl = idx[:, None] >= idx[None, :]
    strict = idx[:, None] > idx[None, :]
    diff = gcum[..., :, None] - gcum[..., None, :]
    k_beta = kc * betac[..., None]
    lower = (jnp.einsum("bhntd,bhnsd->bhnts", k_beta, kc)
             * jnp.exp(jnp.where(strict, diff, -jnp.inf)))
    system = jnp.eye(CHUNK, dtype=jnp.float32) + lower
    u = lax.linalg.triangular_solve(system, vc * betac[..., None], left_side=True,
                                    lower=True, unit_diagonal=True)
    w = lax.linalg.triangular_solve(system, k_beta * jnp.exp(gcum)[..., None], left_side=True,
                                    lower=True, unit_diagonal=True)
    attn = (jnp.einsum("bhntd,bhnsd->bhnts", qc, kc)
            * jnp.exp(jnp.where(incl, diff, -jnp.inf)))
    q_dec = qc * jnp.exp(gcum)[..., None]
    g_last = gcum[..., -1]
    k_dec = kc * jnp.exp(g_last[..., None] - gcum)[..., None]
    scan_in = tuple(jnp.moveaxis(t, 2, 0) for t in (q_dec, k_dec, u, w, attn, g_last))

    def step(state, inp):
        qi, ki, ui, wi, ai, gl = inp
        v_new = ui - jnp.einsum("bhcd,bhdv->bhcv", wi, state)
        o = (jnp.einsum("bhcd,bhdv->bhcv", qi, state)
             + jnp.einsum("bhts,bhsv->bhtv", ai, v_new))
        state = (state * jnp.exp(gl)[..., None, None]
                 + jnp.einsum("bhsd,bhsv->bhdv", ki, v_new))
        return state, o

    s0 = jnp.zeros((b, h, dk, dv), jnp.float32)
    _, o = lax.scan(step, s0, scan_in)
    return jnp.moveaxis(o, 0, 2).reshape(b, h, s, dv)


def hybrid_mixer(h, w_in, lb, hgrn_out_norm, gdn_conv, gdn_a_log, gdn_dt_bias,
                 gdn_out_norm, w_out):
    proj = h @ w_in
    cuts = [D_HGRN, 2 * D_HGRN, 3 * D_HGRN, 4 * D_HGRN,
            4 * D_HGRN + 3 * D_GDN, 4 * D_HGRN + 4 * D_GDN,
            4 * D_HGRN + 4 * D_GDN + N_GDN_HEADS]
    hq, hf, hi, hg, gqkv, gz, ga, gb = jnp.split(proj, cuts, axis=-1)

    f = lb + (1.0 - lb) * jax.nn.sigmoid(hf.astype(jnp.float32))
    o_a = hgrn2_chunked(to_heads(jax.nn.silu(hq)), to_heads(1.0 - f),
                        to_heads(hi), to_heads(jnp.log(f)))
    o_a = from_heads(rms_norm(o_a, hgrn_out_norm)) * jax.nn.silu(hg.astype(jnp.float32))

    qkv = jax.nn.silu(causal_dwconv(gqkv, gdn_conv))
    gq, gk, gv = jnp.split(qkv, 3, axis=-1)
    gq = l2_normalize(to_heads(gq)) * (HEAD_DIM ** -0.5)
    gk = l2_normalize(to_heads(gk))
    log_a = (-jnp.exp(gdn_a_log.astype(jnp.float32))
             * jax.nn.softplus(ga.astype(jnp.float32) + gdn_dt_bias.astype(jnp.float32)))
    beta = jax.nn.sigmoid(gb.astype(jnp.float32))
    o_b = gated_delta_chunked(gq, gk, to_heads(gv), log_a.transpose(0, 2, 1),
                              beta.transpose(0, 2, 1))
    o_b = from_heads(rms_norm(o_b, gdn_out_norm)) * jax.nn.silu(gz.astype(jnp.float32))

    mixed = jnp.concatenate([o_a, o_b], axis=-1).astype(h.dtype)
    return mixed @ w_out


def conv_gated_mlp(h, w_up, ffn_conv, w_down):
    u = causal_dwconv(h @ w_up, ffn_conv)
    gate, up = jnp.split(u, 2, axis=-1)
    return (jax.nn.silu(gate) * up) @ w_down


def setup_inputs(seed: int = 0) -> dict:
    key = jax.random.key(seed)
    ks = jax.random.split(key, 24)
    f32 = jnp.float32

    def nrm(k, shape, scale):
        return jax.random.normal(k, shape, f32) * scale

    def gain(k, shape):
        return 1.0 + 0.05 * jax.random.normal(k, shape, f32)

    dt = jnp.exp(jax.random.uniform(ks[9], (DEPTH, N_GDN_HEADS), f32,
                                    minval=math.log(1e-3), maxval=math.log(1e-1)))
    return {
        "x": nrm(ks[0], (BATCH, SEQ, D_MODEL), 1.0),
        "p": nrm(ks[1], (DEPTH, BATCH, SEQ, PLE_DIM), 1.0),
        "mix_norm_pre": gain(ks[2], (DEPTH, D_MODEL)),
        "mix_norm_post": gain(ks[3], (DEPTH, D_MODEL)),
        "w_in": nrm(ks[4], (DEPTH, D_MODEL, D_IN), D_MODEL ** -0.5),
        "hgrn_lb_logits": nrm(ks[5], (DEPTH + 1, D_HGRN), 0.5),
        "hgrn_out_norm": gain(ks[6], (DEPTH, HEAD_DIM)),
        "gdn_conv": nrm(ks[7], (DEPTH, GDN_CONV, 3 * D_GDN), GDN_CONV ** -0.5),
        "gdn_a_log": jnp.log(jax.random.uniform(ks[8], (DEPTH, N_GDN_HEADS), f32,
                                                minval=1.0, maxval=16.0)),
        "gdn_dt_bias": dt + jnp.log(-jnp.expm1(-dt)),
        "gdn_out_norm": gain(ks[10], (DEPTH, HEAD_DIM)),
        "w_out": nrm(ks[11], (DEPTH, D_MIX, D_MODEL), D_MIX ** -0.5),
        "ffn_norm_pre": gain(ks[12], (DEPTH, D_MODEL)),
        "ffn_norm_post": gain(ks[13], (DEPTH, D_MODEL)),
        "w_up": nrm(ks[14], (DEPTH, D_MODEL, 2 * D_FF), D_MODEL ** -0.5),
        "ffn_conv": nrm(ks[15], (DEPTH, FFN_CONV, 2 * D_FF), FFN_CONV ** -0.5),
        "w_down": nrm(ks[16], (DEPTH, D_FF, D_MODEL), D_FF ** -0.5),
        "w_ple": nrm(ks[17], (DEPTH, PLE_DIM, D_MODEL), PLE_DIM ** -0.5),
        "w_ple_gate": nrm(ks[18], (DEPTH, D_MODEL, D_MODEL), D_MODEL ** -0.5),
        "ple_norm_post": gain(ks[19], (DEPTH, D_MODEL)),
    }


def reference(x, p, mix_norm_pre, mix_norm_post, w_in, hgrn_lb_logits, hgrn_out_norm,
              gdn_conv, gdn_a_log, gdn_dt_bias, gdn_out_norm, w_out, ffn_norm_pre,
              ffn_norm_post, w_up, ffn_conv, w_down, w_ple, w_ple_gate, ple_norm_post):
    lower_bounds = jnp.cumsum(jax.nn.softmax(hgrn_lb_logits.astype(jnp.float32), axis=0),
                              axis=0)
    for i in range(DEPTH):
        h = rms_norm(x, mix_norm_pre[i])
        mix = hybrid_mixer(h, w_in[i], lower_bounds[i], hgrn_out_norm[i], gdn_conv[i],
                           gdn_a_log[i], gdn_dt_bias[i], gdn_out_norm[i], w_out[i])
        x = x + rms_norm(mix, mix_norm_post[i])
        h = rms_norm(x, ffn_norm_pre[i])
        x = x + rms_norm(conv_gated_mlp(h, w_up[i], ffn_conv[i], w_down[i]), ffn_norm_post[i])
        ple = (p[i] @ w_ple[i]) * jax.nn.sigmoid(x @ w_ple_gate[i])
        x = x + rms_norm(ple, ple_norm_post[i])
    return x
```

```python
import functools

import jax
import jax.numpy as jnp
from jax import lax
from jax.experimental import pallas as pl
from jax.experimental.pallas import tpu as pltpu

F32 = jnp.float32
BF16 = jnp.bfloat16

HEAD_DIM = 128
N_HEADS = 4
D_GROUP = N_HEADS * HEAD_DIM
GDN_CONV = 4
FFN_CONV = 3
CHUNK = 64
SUB = 8
EPS = 1e-6
NEG_BIG = -1e30
HALO = 8

V7X_VMEM_LIMIT_BYTES = 60000 * 1024

IN_TILE = 512
MIX_TILE = 256
POST_TILE = 512
FF_TILE = 256


def _dot(a, b):
    return jnp.dot(a, b, preferred_element_type=F32)


def _dot_nt(a, b):
    return lax.dot_general(a, b, (((1,), (1,)), ((), ())), preferred_element_type=F32)


def _dot_tn(a, b):
    return lax.dot_general(a, b, (((0,), (0,)), ((), ())), preferred_element_type=F32)


def _split2(x):
    hi = x.astype(BF16)
    lo = (x - hi.astype(F32)).astype(BF16)
    return hi, lo


def _split3(x):
    hi = x.astype(BF16)
    r = x - hi.astype(F32)
    mid = r.astype(BF16)
    lo = (r - mid.astype(F32)).astype(BF16)
    return hi, mid, lo


def _mm3(a, b):
    ah, al = _split2(a)
    bh, bl = _split2(b)
    return _dot(ah, bh) + (_dot(ah, bl) + _dot(al, bh))


def _sigmoid(x):
    return 1.0 / (1.0 + jnp.exp(-x))


def _silu(x):
    return x * _sigmoid(x)


def _softplus(x):
    return jnp.maximum(x, 0.0) + jnp.log(1.0 + jnp.exp(-jnp.abs(x)))


def _rms_norm(x, w):
    return x * lax.rsqrt(jnp.mean(x * x, axis=-1, keepdims=True) + EPS) * w


def _const_spec(shape):
    nd = len(shape)
    return pl.BlockSpec(shape, lambda *_: (0,) * nd, pipeline_mode=pl.Buffered(1))


def _in_proj_kernel(tiles_per_seq, x_ref, nw_ref, w_ref, wab_ref, lbl_ref, cw_ref, alog_ref, dtb_ref,
                    hq_ref, g_ref, hi_ref, hg_ref, gq_ref, gk_ref, gv_ref, gz_ref, ab_ref,
                    cbuf_ref):
    tm = x_ref.shape[0]
    i = pl.program_id(0)

    @pl.when(i % tiles_per_seq == 0)
    def _():
        cbuf_ref[0:HALO, :] = jnp.zeros((HALO, 3 * D_GROUP), F32)

    h = _rms_norm(x_ref[...], nw_ref[...]).astype(BF16)

    def proj(group):
        return _dot(h, w_ref[:, group * D_GROUP:(group + 1) * D_GROUP])

    hq_ref[...] = _silu(proj(0)).astype(BF16)

    l0 = lbl_ref[0:1, :]
    l1 = lbl_ref[1:2, :]
    lmax = jnp.maximum(l0, l1)
    e0 = jnp.exp(l0 - lmax)
    e1 = jnp.exp(l1 - lmax)
    lb = e0 / (e0 + e1)
    f = lb + (1.0 - lb) * _sigmoid(proj(1))
    g_ref[...] = jnp.log(f)

    hi_ref[...] = proj(2).astype(BF16)
    hg_ref[...] = _silu(proj(3)).astype(BF16)

    for j in range(3):
        cbuf_ref[HALO:HALO + tm, j * D_GROUP:(j + 1) * D_GROUP] = proj(4 + j)
    outs = (gq_ref, gk_ref, gv_ref)
    for j in range(3):
        cols = slice(j * D_GROUP, (j + 1) * D_GROUP)
        acc = cw_ref[GDN_CONV - 1:GDN_CONV, cols] * cbuf_ref[HALO:HALO + tm, cols]
        for k in range(GDN_CONV - 1):
            off = HALO - (GDN_CONV - 1) + k
            acc = acc + cw_ref[k:k + 1, cols] * cbuf_ref[off:off + tm, cols]
        y = _silu(acc)
        if j < 2:
            scale = HEAD_DIM ** -0.5 if j == 0 else 1.0
            for hd in range(N_HEADS):
                yh = y[:, hd * HEAD_DIM:(hd + 1) * HEAD_DIM]
                nrm = lax.rsqrt(jnp.sum(yh * yh, axis=-1, keepdims=True) + EPS) * scale
                outs[j][:, hd * HEAD_DIM:(hd + 1) * HEAD_DIM] = (yh * nrm).astype(BF16)
        else:
            outs[j][...] = y.astype(BF16)
    cbuf_ref[0:HALO, :] = cbuf_ref[tm:tm + HALO, :]

    gz_ref[...] = _silu(proj(7)).astype(BF16)

    z = _dot(h, wab_ref[...])
    lane = lax.broadcasted_iota(jnp.int32, z.shape, 1)
    log_a = -jnp.exp(alog_ref[...]) * _softplus(z + dtb_ref[...])
    ab_ref[...] = jnp.where(lane < N_HEADS, log_a, _sigmoid(z))


def _in_proj(x2d, norm_w, w_main, w_ab, lb_logits, conv_w, a_log_pad, dt_bias_pad, seq_len, tm):
    n_tok, d_model = x2d.shape
    assert seq_len % tm == 0 and n_tok % seq_len == 0
    grid = (n_tok // tm,)
    row = lambda i: (i, 0)
    tok_spec = lambda width: pl.BlockSpec((tm, width), row)
    out_shapes = [jax.ShapeDtypeStruct((n_tok, D_GROUP), BF16) for _ in range(8)]
    out_shapes[1] = jax.ShapeDtypeStruct((n_tok, D_GROUP), F32)
    out_shapes.append(jax.ShapeDtypeStruct((n_tok, HEAD_DIM), F32))
    return pl.pallas_call(
        functools.partial(_in_proj_kernel, seq_len // tm),
        grid=grid,
        in_specs=[tok_spec(d_model), _const_spec(norm_w.shape), _const_spec(w_main.shape),
                  _const_spec(w_ab.shape), _const_spec(lb_logits.shape), _const_spec(conv_w.shape),
                  _const_spec(a_log_pad.shape), _const_spec(dt_bias_pad.shape)],
        out_specs=[tok_spec(D_GROUP)] * 8 + [tok_spec(HEAD_DIM)],
        out_shape=out_shapes,
        scratch_shapes=[pltpu.VMEM((tm + HALO, 3 * D_GROUP), F32)],
        compiler_params=pltpu.CompilerParams(dimension_semantics=("arbitrary",),
                                             vmem_limit_bytes=V7X_VMEM_LIMIT_BYTES),
        name="in_proj",
    )(x2d, norm_w, w_main, w_ab, lb_logits, conv_w, a_log_pad, dt_bias_pad)


def _iota2(shape, dim):
    return lax.broadcasted_iota(jnp.int32, shape, dim)


def _hgrn_scores(q, kk, bc):
    c, d = q.shape
    t_idx = _iota2((c, c), 0)
    s_idx = _iota2((c, c), 1)
    row = _iota2((c, 1), 0)
    scores = jnp.zeros((c, c), F32)
    m = c // 2
    while m >= SUB:
        pieces = []
        for b in range(c // (2 * m)):
            r = b * 2 * m + m - 1
            pieces.append(jnp.broadcast_to(bc[r:r + 1, :], (2 * m, d)))
        ref = pieces[0] if len(pieces) == 1 else jnp.concatenate(pieces, axis=0)
        upper = (row & m) != 0
        diff = bc - ref
        e = jnp.exp(jnp.where(upper, diff, -diff))
        qt = jnp.where(upper, q * e, 0.0).astype(BF16)
        kt = jnp.where(upper, 0.0, kk * e).astype(BF16)
        p = _dot_nt(qt, kt)
        scores = scores + jnp.where((t_idx ^ s_idx) < 2 * m, p, 0.0)
        m //= 2

    lane = _iota2((SUB, c), 1)
    tiles = []
    for r in range(c // SUB):
        blk_bc = bc[r * SUB:(r + 1) * SUB, :]
        blk_q = q[r * SUB:(r + 1) * SUB, :]
        tile = jnp.zeros((SUB, c), F32)
        for sl in range(SUB):
            s = r * SUB + sl
            dec = jnp.exp(jnp.minimum(blk_bc - bc[s:s + 1, :], 0.0))
            val = jnp.sum(blk_q * kk[s:s + 1, :] * dec, axis=-1, keepdims=True)
            tile = jnp.where(lane == s, val, tile)
        tiles.append(tile)
    diag = jnp.concatenate(tiles, axis=0)
    return scores + jnp.where((t_idx >= s_idx) & ((t_idx ^ s_idx) < SUB), diag, 0.0)


def _hgrn_head(q_bf, g, bc, v_bf, st):
    c = q_bf.shape[0]
    q = q_bf.astype(F32)
    kk = 1.0 - jnp.exp(g)
    b_last = bc[c - 1:c, :]
    q_dec = (q * jnp.exp(bc)).astype(BF16)
    k_dec = (kk * jnp.exp(b_last - bc)).astype(BF16)
    scores = _hgrn_scores(q, kk, bc)
    o = _dot_nt(q_dec, st.astype(BF16)) + _dot(scores.astype(BF16), v_bf)
    st_new = st * jnp.exp(b_last) + _dot_tn(v_bf, k_dec)
    return o, st_new


def _unit_lower_inverse(a):
    c = a.shape[0]
    t_idx = _iota2((c, c), 0)
    s_idx = _iota2((c, c), 1)
    eye = (t_idx == s_idx).astype(F32)
    d1 = jnp.where((t_idx ^ s_idx) < SUB, a, 0.0)
    d2 = _mm3(d1, d1)
    d4 = _mm3(d2, d2)
    t = eye - d1
    t = t + _mm3(t, d2)
    t = t + _mm3(t, d4)
    m = SUB
    while m < c:
        x_idx = t_idx ^ s_idx
        cm = jnp.where((x_idx < 2 * m) & (x_idx >= m), a, 0.0)
        t = t - _mm3(t, _mm3(cm, t))
        m *= 2
    return t


def _gdn_head(q_bf, k_bf, v_bf, beta, gc, gc_row, s):
    c = q_bf.shape[0]
    t_idx = _iota2((c, c), 0)
    s_idx = _iota2((c, c), 1)
    k = k_bf.astype(F32)
    kb = k * beta
    decay = jnp.exp(jnp.minimum(gc - gc_row, 0.0))
    both = _dot_nt(jnp.concatenate([kb.astype(BF16), q_bf], axis=0), k_bf)
    a = jnp.where(t_idx > s_idx, both[:c] * decay, 0.0)
    attn = jnp.where(t_idx >= s_idx, both[c:] * decay, 0.0)
    t = _unit_lower_inverse(a)
    e_gc = jnp.exp(gc)
    rhs = jnp.concatenate([v_bf.astype(F32) * beta, kb * e_gc], axis=1)
    uw = _dot(t.astype(BF16), rhs.astype(BF16))
    d = k.shape[1]
    u, w = uw[:, :d], uw[:, d:]
    q_dec = (q_bf.astype(F32) * e_gc).astype(BF16)
    ws_qs = _dot(jnp.concatenate([w.astype(BF16), q_dec], axis=0), s.astype(BF16))
    v_new = u - ws_qs[:c]
    v_new_bf = v_new.astype(BF16)
    o = ws_qs[c:] + _dot(attn.astype(BF16), v_new_bf)
    g_last = gc[c - 1:c, :]
    k_dec = (k * jnp.exp(g_last - gc)).astype(BF16)
    s_new = s * jnp.exp(g_last) + _dot_tn(k_dec, v_new_bf)
    return o, s_new


def _mixer_kernel(hq_ref, g_ref, hi_ref, hg_ref, gq_ref, gk_ref, gv_ref, gz_ref, ab_ref, hn_ref, gn_ref,
                  out_ref, hst_ref, gst_ref):
    tb = hq_ref.shape[0]
    c = CHUNK

    @pl.when(pl.program_id(1) == 0)
    def _():
        hst_ref[...] = jnp.zeros(hst_ref.shape, F32)
        gst_ref[...] = jnp.zeros(gst_ref.shape, F32)

    tri = (_iota2((c, c), 0) >= _iota2((c, c), 1)).astype(BF16)
    pick = (_iota2((SUB, HEAD_DIM), 0) == _iota2((SUB, HEAD_DIM), 1)).astype(BF16)

    def cumsum_rows(x):
        hi, mid, lo = _split3(x)
        return _dot(tri, hi) + (_dot(tri, mid) + _dot(tri, lo))

    def chunk_body(ci, carry):
        rows = pl.ds(pl.multiple_of(ci * c, c), c)
        g_all = g_ref[rows, :]
        bc_all = cumsum_rows(g_all)
        ab = ab_ref[rows, :]
        gcum = cumsum_rows(ab)
        hi, mid, lo = _split3(gcum)
        gcum_rows = _dot_nt(pick, hi) + (_dot_nt(pick, mid) + _dot_nt(pick, lo))
        for hd in range(N_HEADS):
            cols = slice(hd * HEAD_DIM, (hd + 1) * HEAD_DIM)
            o, st_new = _hgrn_head(hq_ref[rows, cols], g_all[:, cols], bc_all[:, cols], hi_ref[rows, cols],
                                   hst_ref[hd])
            hst_ref[hd] = st_new
            o = _rms_norm(o, hn_ref[...]) * hg_ref[rows, cols].astype(F32)
            out_ref[rows, cols] = o.astype(BF16)
        for hd in range(N_HEADS):
            cols = slice(hd * HEAD_DIM, (hd + 1) * HEAD_DIM)
            o, s_new = _gdn_head(gq_ref[rows, cols], gk_ref[rows, cols], gv_ref[rows, cols],
                                 ab[:, N_HEADS + hd:N_HEADS + hd + 1], gcum[:, hd:hd + 1],
                                 gcum_rows[hd:hd + 1, :], gst_ref[hd])
            gst_ref[hd] = s_new
            o = _rms_norm(o, gn_ref[...]) * gz_ref[rows, cols].astype(F32)
            out_ref[rows, D_GROUP + hd * HEAD_DIM:D_GROUP + (hd + 1) * HEAD_DIM] = o.astype(BF16)
        return carry

    lax.fori_loop(0, tb // c, chunk_body, 0)


def _mixer(hq, g, hi, hg, gq, gk, gv, gz, ab, hgrn_norm, gdn_norm, batch, seq_len, tb):
    n_tok = hq.shape[0]
    assert n_tok == batch * seq_len and seq_len % tb == 0 and tb % CHUNK == 0
    steps = seq_len // tb
    row = lambda b, i: (b * steps + i, 0)
    tok_spec = lambda width: pl.BlockSpec((tb, width), row)
    return pl.pallas_call(
        _mixer_kernel,
        grid=(batch, steps),
        in_specs=[tok_spec(D_GROUP)] * 8 + [tok_spec(HEAD_DIM),
                                            _const_spec(hgrn_norm.shape), _const_spec(gdn_norm.shape)],
        out_specs=tok_spec(2 * D_GROUP),
        out_shape=jax.ShapeDtypeStruct((n_tok, 2 * D_GROUP), BF16),
        scratch_shapes=[pltpu.VMEM((N_HEADS, HEAD_DIM, HEAD_DIM), F32),
                        pltpu.VMEM((N_HEADS, HEAD_DIM, HEAD_DIM), F32)],
        compiler_params=pltpu.CompilerParams(dimension_semantics=("arbitrary", "arbitrary"),
                                             vmem_limit_bytes=V7X_VMEM_LIMIT_BYTES),
        name="mixer",
    )(hq, g, hi, hg, gq, gk, gv, gz, ab, hgrn_norm, gdn_norm)


def _post_kernel(tiles_per_seq, x_ref, mix_ref, p_ref, wout_ref, wup_ref, cw_ref, wdown_ref, wple_ref,
                 wgate_ref, n_mix_ref, n_pre_ref, n_post_ref, n_ple_ref, out_ref,
                 ubuf_ref, halo_ref, acc_ref):
    tm = x_ref.shape[0]
    d_ff = wdown_ref.shape[0]
    i = pl.program_id(0)

    @pl.when(i % tiles_per_seq == 0)
    def _():
        halo_ref[...] = jnp.zeros(halo_ref.shape, F32)

    x1 = x_ref[...] + _rms_norm(_dot(mix_ref[...], wout_ref[...]), n_mix_ref[...])
    h2 = _rms_norm(x1, n_pre_ref[...]).astype(BF16)

    def conv(cols):
        width = cols.stop - cols.start
        u = _dot(h2, wup_ref[:, cols])
        ubuf_ref[0:HALO, 0:width] = halo_ref[:, cols]
        ubuf_ref[HALO:HALO + tm, 0:width] = u
        halo_ref[:, cols] = u[tm - HALO:tm, :]
        out = cw_ref[FFN_CONV - 1:FFN_CONV, cols] * u
        for k in range(FFN_CONV - 1):
            off = HALO - (FFN_CONV - 1) + k
            out = out + cw_ref[k:k + 1, cols] * ubuf_ref[off:off + tm, 0:width]
        return out

    for j in range(d_ff // FF_TILE):
        gate = conv(slice(j * FF_TILE, (j + 1) * FF_TILE))
        up = conv(slice(d_ff + j * FF_TILE, d_ff + (j + 1) * FF_TILE))
        act = (_silu(gate) * up).astype(BF16)
        part = _dot(act, wdown_ref[j * FF_TILE:(j + 1) * FF_TILE, :])
        if j == 0:
            acc_ref[...] = part
        else:
            acc_ref[...] += part

    x2 = x1 + _rms_norm(acc_ref[...], n_post_ref[...])
    ple = _dot(p_ref[...].astype(BF16), wple_ref[...]) * _sigmoid(_dot(x2.astype(BF16), wgate_ref[...]))
    out_ref[...] = x2 + _rms_norm(ple, n_ple_ref[...])


def _post(x2d, mixed, p2d, w_out, w_up, ffn_conv, w_down, w_ple, w_gate, n_mix, n_pre, n_post, n_ple,
          seq_len, tm):
    n_tok, d_model = x2d.shape
    d_ff = w_down.shape[0]
    assert seq_len % tm == 0 and d_ff % FF_TILE == 0
    row = lambda i: (i, 0)
    tok_spec = lambda width: pl.BlockSpec((tm, width), row)
    consts = (w_out, w_up, ffn_conv, w_down, w_ple, w_gate, n_mix, n_pre, n_post, n_ple)
    return pl.pallas_call(
        functools.partial(_post_kernel, seq_len // tm),
        grid=(n_tok // tm,),
        in_specs=[tok_spec(d_model), tok_spec(mixed.shape[1]), tok_spec(p2d.shape[1])]
                 + [_const_spec(a.shape) for a in consts],
        out_specs=tok_spec(d_model),
        out_shape=jax.ShapeDtypeStruct((n_tok, d_model), F32),
        scratch_shapes=[pltpu.VMEM((tm + HALO, FF_TILE), F32),
                        pltpu.VMEM((HALO, 2 * d_ff), F32),
                        pltpu.VMEM((tm, d_model), F32)],
        compiler_params=pltpu.CompilerParams(dimension_semantics=("arbitrary",),
                                             vmem_limit_bytes=V7X_VMEM_LIMIT_BYTES),
        name="post",
    )(x2d, mixed, p2d, *consts)


def _layer(x2d, p2d, mix_norm_pre, mix_norm_post, w_in, lb_logits, hgrn_out_norm, gdn_conv, gdn_a_log,
           gdn_dt_bias, gdn_out_norm, w_out, ffn_norm_pre, ffn_norm_post, w_up, ffn_conv, w_down, w_ple,
           w_ple_gate, ple_norm_post, batch, seq_len, in_tile, mix_tile, post_tile):
    n_main = 8 * D_GROUP
    n_ab = 2 * N_HEADS
    assert w_in.shape[1] == n_main + n_ab
    row = lambda v: v.reshape(1, -1).astype(F32)
    pad_lanes = lambda v: jnp.pad(row(v), ((0, 0), (0, HEAD_DIM - v.shape[-1])))
    w_main = w_in[:, :n_main].astype(BF16)
    w_ab = jnp.pad(w_in[:, n_main:], ((0, 0), (0, HEAD_DIM - n_ab))).astype(BF16)
    hq, g, hi, hg, gq, gk, gv, gz, ab = _in_proj(
        x2d, row(mix_norm_pre), w_main, w_ab, lb_logits.astype(F32), gdn_conv.astype(F32),
        pad_lanes(gdn_a_log), pad_lanes(gdn_dt_bias), seq_len, in_tile)
    mixed = _mixer(hq, g, hi, hg, gq, gk, gv, gz, ab, row(hgrn_out_norm), row(gdn_out_norm),
                   batch, seq_len, mix_tile)
    return _post(x2d, mixed, p2d, w_out.astype(BF16), w_up.astype(BF16), ffn_conv.astype(F32),
                 w_down.astype(BF16), w_ple.astype(BF16), w_ple_gate.astype(BF16),
                 row(mix_norm_post), row(ffn_norm_pre), row(ffn_norm_post), row(ple_norm_post),
                 seq_len, post_tile)


def kernel(x, p, mix_norm_pre, mix_norm_post, w_in, hgrn_lb_logits, hgrn_out_norm, gdn_conv, gdn_a_log,
           gdn_dt_bias, gdn_out_norm, w_out, ffn_norm_pre, ffn_norm_post, w_up, ffn_conv, w_down, w_ple,
           w_ple_gate, ple_norm_post):
    batch, seq_len, d_model = x.shape
    depth = p.shape[0]
    assert depth == 1 and hgrn_lb_logits.shape[0] == 2
    x2d = x.reshape(batch * seq_len, d_model)
    for i in range(depth):
        x2d = _layer(x2d, p[i].reshape(batch * seq_len, -1), mix_norm_pre[i], mix_norm_post[i], w_in[i],
                     hgrn_lb_logits, hgrn_out_norm[i], gdn_conv[i], gdn_a_log[i], gdn_dt_bias[i],
                     gdn_out_norm[i], w_out[i], ffn_norm_pre[i], ffn_norm_post[i], w_up[i], ffn_conv[i],
                     w_down[i], w_ple[i], w_ple_gate[i], ple_norm_post[i],
                     batch, seq_len, IN_TILE, MIX_TILE, POST_TILE)
    return x2d.reshape(batch, seq_len, d_model)
```

```python
import functools

import jax
import jax.numpy as jnp
from jax import lax
from jax.experimental import pallas as pl
from jax.experimental.pallas import tpu as pltpu

F32 = jnp.float32
BF16 = jnp.bfloat16

HEAD_DIM = 128
N_HEADS = 4
D_GROUP = N_HEADS * HEAD_DIM
GDN_CONV = 4
FFN_CONV = 3
CHUNK = 64
SUB = 8
EPS = 1e-6
NEG_BIG = -1e30
HALO = 8

V7X_VMEM_LIMIT_BYTES = 60000 * 1024

IN_TILE = 512
MIX_TILE = 256
POST_TILE = 512
FF_TILE = 256


def _dot(a, b):
    return jnp.dot(a, b, preferred_element_type=F32)


def _dot_nt(a, b):
    return lax.dot_general(a, b, (((1,), (1,)), ((), ())), preferred_element_type=F32)


def _dot_tn(a, b):
    return lax.dot_general(a, b, (((0,), (0,)), ((), ())), preferred_element_type=F32)


def _split2(x):
    hi = x.astype(BF16)
    lo = (x - hi.astype(F32)).astype(BF16)
    return hi, lo


def _split3(x):
    hi = x.astype(BF16)
    r = x - hi.astype(F32)
    mid = r.astype(BF16)
    lo = (r - mid.astype(F32)).astype(BF16)
    return hi, mid, lo


def _mm3(a, b):
    ah, al = _split2(a)
    bh, bl = _split2(b)
    return _dot(ah, bh) + (_dot(ah, bl) + _dot(al, bh))


def _sigmoid(x):
    return 1.0 / (1.0 + jnp.exp(-x))


def _silu(x):
    return x * _sigmoid(x)


def _softplus(x):
    return jnp.maximum(x, 0.0) + jnp.log(1.0 + jnp.exp(-jnp.abs(x)))


def _rms_norm(x, w):
    return x * lax.rsqrt(jnp.mean(x * x, axis=-1, keepdims=True) + EPS) * w


def _const_spec(shape):
    nd = len(shape)
    return pl.BlockSpec(shape, lambda *_: (0,) * nd, pipeline_mode=pl.Buffered(1))


def _in_proj_kernel(tiles_per_seq, x_ref, nw_ref, w_ref, wab_ref, lbl_ref, cw_ref, alog_ref, dtb_ref,
                    hq_ref, g_ref, hi_ref, hg_ref, gq_ref, gk_ref, gv_ref, gz_ref, ab_ref,
                    cbuf_ref):
    tm = x_ref.shape[0]
    i = pl.program_id(0)

    @pl.when(i % tiles_per_seq == 0)
    def _():
        cbuf_ref[0:HALO, :] = jnp.zeros((HALO, 3 * D_GROUP), F32)

    h = _rms_norm(x_ref[...], nw_ref[...]).astype(BF16)

    def proj(group):
        return _dot(h, w_ref[:, group * D_GROUP:(group + 1) * D_GROUP])

    hq_ref[...] = _silu(proj(0)).astype(BF16)

    l0 = lbl_ref[0:1, :]
    l1 = lbl_ref[1:2, :]
    lmax = jnp.maximum(l0, l1)
    e0 = jnp.exp(l0 - lmax)
    e1 = jnp.exp(l1 - lmax)
    lb = e0 / (e0 + e1)
    f = lb + (1.0 - lb) * _sigmoid(proj(1))
    g_ref[...] = jnp.log(f)

    hi_ref[...] = proj(2).astype(BF16)
    hg_ref[...] = _silu(proj(3)).astype(BF16)

    for j in range(3):
        cbuf_ref[HALO:HALO + tm, j * D_GROUP:(j + 1) * D_GROUP] = proj(4 + j)
    outs = (gq_ref, gk_ref, gv_ref)
    for j in range(3):
        cols = slice(j * D_GROUP, (j + 1) * D_GROUP)
        acc = cw_ref[GDN_CONV - 1:GDN_CONV, cols] * cbuf_ref[HALO:HALO + tm, cols]
        for k in range(GDN_CONV - 1):
            off = HALO - (GDN_CONV - 1) + k
            acc = acc + cw_ref[k:k + 1, cols] * cbuf_ref[off:off + tm, cols]
        y = _silu(acc)
        if j < 2:
            scale = HEAD_DIM ** -0.5 if j == 0 else 1.0
            for hd in range(N_HEADS):
                yh = y[:, hd * HEAD_DIM:(hd + 1) * HEAD_DIM]
                nrm = lax.rsqrt(jnp.sum(yh * yh, axis=-1, keepdims=True) + EPS) * scale
                outs[j][:, hd * HEAD_DIM:(hd + 1) * HEAD_DIM] = (yh * nrm).astype(BF16)
        else:
            outs[j][...] = y.astype(BF16)
    cbuf_ref[0:HALO, :] = cbuf_ref[tm:tm + HALO, :]

    gz_ref[...] = _silu(proj(7)).astype(BF16)

    z = _dot(h, wab_ref[...])
    lane = lax.broadcasted_iota(jnp.int32, z.shape, 1)
    log_a = -jnp.exp(alog_ref[...]) * _softplus(z + dtb_ref[...])
    ab_ref[...] = jnp.where(lane < N_HEADS, log_a, _sigmoid(z))


def _in_proj(x2d, norm_w, w_main, w_ab, lb_logits, conv_w, a_log_pad, dt_bias_pad, seq_len, tm):
    n_tok, d_model = x2d.shape
    assert seq_len % tm == 0 and n_tok % seq_len == 0
    grid = (n_tok // tm,)
    row = lambda i: (i, 0)
    tok_spec = lambda width: pl.BlockSpec((tm, width), row)
    out_shapes = [jax.ShapeDtypeStruct((n_tok, D_GROUP), BF16) for _ in range(8)]
    out_shapes[1] = jax.ShapeDtypeStruct((n_tok, D_GROUP), F32)
    out_shapes.append(jax.ShapeDtypeStruct((n_tok, HEAD_DIM), F32))
    return pl.pallas_call(
        functools.partial(_in_proj_kernel, seq_len // tm),
        grid=grid,
        in_specs=[tok_spec(d_model), _const_spec(norm_w.shape), _const_spec(w_main.shape),
                  _const_spec(w_ab.shape), _const_spec(lb_logits.shape), _const_spec(conv_w.shape),
                  _const_spec(a_log_pad.shape), _const_spec(dt_bias_pad.shape)],
        out_specs=[tok_spec(D_GROUP)] * 8 + [tok_spec(HEAD_DIM)],
        out_shape=out_shapes,
        scratch_shapes=[pltpu.VMEM((tm + HALO, 3 * D_GROUP), F32)],
        compiler_params=pltpu.CompilerParams(dimension_semantics=("arbitrary",),
                                             vmem_limit_bytes=V7X_VMEM_LIMIT_BYTES),
        name="in_proj",
    )(x2d, norm_w, w_main, w_ab, lb_logits, conv_w, a_log_pad, dt_bias_pad)


def _iota2(shape, dim):
    return lax.broadcasted_iota(jnp.int32, shape, dim)


def _hgrn_level_operands(q, kk, bc, m):
    c, d = q.shape
    row = _iota2((c, 1), 0)
    pieces = []
    for b in range(c // (2 * m)):
        r = b * 2 * m + m - 1
        pieces.append(jnp.broadcast_to(bc[r:r + 1, :], (2 * m, d)))
    ref = pieces[0] if len(pieces) == 1 else jnp.concatenate(pieces, axis=0)
    upper = (row & m) != 0
    diff = bc - ref
    e = jnp.exp(jnp.where(upper, diff, -diff))
    qt = jnp.where(upper, q * e, 0.0).astype(BF16)
    kt = jnp.where(upper, 0.0, kk * e).astype(BF16)
    return qt, kt


def _hgrn_diag_scores(q, kk, bc):
    c = q.shape[0]
    lane = _iota2((SUB, c), 1)
    tiles = []
    for r in range(c // SUB):
        blk_bc = bc[r * SUB:(r + 1) * SUB, :]
        blk_q = q[r * SUB:(r + 1) * SUB, :]
        tile = jnp.zeros((SUB, c), F32)
        for sl in range(SUB):
            s = r * SUB + sl
            dec = jnp.exp(jnp.minimum(blk_bc - bc[s:s + 1, :], 0.0))
            val = jnp.sum(blk_q * kk[s:s + 1, :] * dec, axis=-1, keepdims=True)
            tile = jnp.where(lane == s, val, tile)
        tiles.append(tile)
    return jnp.concatenate(tiles, axis=0)


def _hgrn_heads(items):
    c = items[0][0].shape[0]
    t_idx = _iota2((c, c), 0)
    s_idx = _iota2((c, c), 1)
    x_idx = t_idx ^ s_idx
    qs_bf, gs, bcs, vs, sts = zip(*items)
    qs = [q.astype(F32) for q in qs_bf]
    kks = [1.0 - jnp.exp(g) for g in gs]
    b_lasts = [bc[c - 1:c, :] for bc in bcs]
    q_decs = [(q * jnp.exp(bc)).astype(BF16) for q, bc in zip(qs, bcs)]
    k_decs = [(kk * jnp.exp(bl - bc)).astype(BF16) for kk, bl, bc in zip(kks, b_lasts, bcs)]
    inter = [_dot_nt(qd, st.astype(BF16)) for qd, st in zip(q_decs, sts)]
    new_sts = [st * jnp.exp(bl) + _dot_tn(v, kd) for st, bl, v, kd in zip(sts, b_lasts, vs, k_decs)]
    scores = [jnp.where((t_idx >= s_idx) & (x_idx < SUB), _hgrn_diag_scores(q, kk, bc), 0.0)
              for q, kk, bc in zip(qs, kks, bcs)]
    m = c // 2
    while m >= SUB:
        operands = [_hgrn_level_operands(q, kk, bc, m) for q, kk, bc in zip(qs, kks, bcs)]
        prods = [_dot_nt(qt, kt) for qt, kt in operands]
        scores = [sc + jnp.where(x_idx < 2 * m, p, 0.0) for sc, p in zip(scores, prods)]
        m //= 2
    outs = [o + _dot(sc.astype(BF16), v) for o, sc, v in zip(inter, scores, vs)]
    return list(zip(outs, new_sts))


def _unit_lower_inverses(mats):
    c = mats[0].shape[0]
    t_idx = _iota2((c, c), 0)
    s_idx = _iota2((c, c), 1)
    x_idx = t_idx ^ s_idx
    eye = (t_idx == s_idx).astype(F32)
    d1 = [jnp.where(x_idx < SUB, a, 0.0) for a in mats]
    d2 = [_mm3(x, x) for x in d1]
    d4 = [_mm3(x, x) for x in d2]
    ts = [eye - x for x in d1]
    ts = [t + _mm3(t, x) for t, x in zip(ts, d2)]
    ts = [t + _mm3(t, x) for t, x in zip(ts, d4)]
    m = SUB
    while m < c:
        cms = [jnp.where((x_idx < 2 * m) & (x_idx >= m), a, 0.0) for a in mats]
        inner = [_mm3(cm, t) for cm, t in zip(cms, ts)]
        ts = [t - _mm3(t, x) for t, x in zip(ts, inner)]
        m *= 2
    return ts


def _gdn_heads(items):
    c, d = items[0][0].shape
    t_idx = _iota2((c, c), 0)
    s_idx = _iota2((c, c), 1)
    n = len(items)
    qs, ks_bf, vs, betas, gcs, gc_rows, states = zip(*items)
    ks = [k.astype(F32) for k in ks_bf]
    kbs = [k * b for k, b in zip(ks, betas)]
    decays = [jnp.exp(jnp.minimum(gc - gr, 0.0)) for gc, gr in zip(gcs, gc_rows)]
    boths = [_dot_nt(jnp.concatenate([kb.astype(BF16), q], axis=0), k) for kb, q, k in zip(kbs, qs, ks_bf)]
    a_mats = [jnp.where(t_idx > s_idx, bo[:c] * dc, 0.0) for bo, dc in zip(boths, decays)]
    attns = [jnp.where(t_idx >= s_idx, bo[c:] * dc, 0.0).astype(BF16) for bo, dc in zip(boths, decays)]
    ts = _unit_lower_inverses(a_mats)
    e_gcs = [jnp.exp(gc) for gc in gcs]
    rhss = [jnp.concatenate([v.astype(F32) * b, kb * e], axis=1).astype(BF16)
            for v, b, kb, e in zip(vs, betas, kbs, e_gcs)]
    uws = [_dot(t.astype(BF16), r) for t, r in zip(ts, rhss)]
    q_decs = [(q.astype(F32) * e).astype(BF16) for q, e in zip(qs, e_gcs)]
    ws_qs = [_dot(jnp.concatenate([uw[:, d:].astype(BF16), qd], axis=0), s.astype(BF16))
             for uw, qd, s in zip(uws, q_decs, states)]
    v_news = [(uw[:, :d] - wq[:c]).astype(BF16) for uw, wq in zip(uws, ws_qs)]
    outs = [wq[c:] + _dot(at, vn) for wq, at, vn in zip(ws_qs, attns, v_news)]
    g_lasts = [gc[c - 1:c, :] for gc in gcs]
    k_decs = [(k * jnp.exp(gl - gc)).astype(BF16) for k, gl, gc in zip(ks, g_lasts, gcs)]
    new_states = [s * jnp.exp(gl) + _dot_tn(kd, vn) for s, gl, kd, vn in zip(states, g_lasts, k_decs, v_news)]
    return [(outs[i], new_states[i]) for i in range(n)]


def _mixer_kernel(hq_ref, g_ref, hi_ref, hg_ref, gq_ref, gk_ref, gv_ref, gz_ref, ab_ref, hn_ref, gn_ref,
                  out_ref, hst_ref, gst_ref):
    n_batch, tb, _ = hq_ref.shape
    c = CHUNK

    @pl.when(pl.program_id(0) == 0)
    def _():
        hst_ref[...] = jnp.zeros(hst_ref.shape, F32)
        gst_ref[...] = jnp.zeros(gst_ref.shape, F32)

    tri = (_iota2((c, c), 0) >= _iota2((c, c), 1)).astype(BF16)
    pick = (_iota2((SUB, HEAD_DIM), 0) == _iota2((SUB, HEAD_DIM), 1)).astype(BF16)

    def cumsum_rows(x):
        hi, mid, lo = _split3(x)
        return _dot(tri, hi) + (_dot(tri, mid) + _dot(tri, lo))

    def head_cols(hd):
        return slice(hd * HEAD_DIM, (hd + 1) * HEAD_DIM)

    def chunk_body(ci, carry):
        rows = pl.ds(pl.multiple_of(ci * c, c), c)
        loaded = [[r[b, rows, :] for r in (hq_ref, g_ref, hi_ref, hg_ref, gq_ref, gk_ref, gv_ref, gz_ref, ab_ref)]
                  for b in range(n_batch)]
        h_states = [[hst_ref[b, hd] for hd in range(N_HEADS)] for b in range(n_batch)]
        g_states = [[gst_ref[b, hd] for hd in range(N_HEADS)] for b in range(n_batch)]
        hn = hn_ref[...]
        gn = gn_ref[...]
        hgrn_items, gdn_items = [], []
        for b in range(n_batch):
            hq, g_all, hi, hg, gq, gk, gv, gz, ab = loaded[b]
            bc_all = cumsum_rows(g_all)
            gcum = cumsum_rows(ab)
            g_hi, g_mid, g_lo = _split3(gcum)
            gcum_rows = _dot_nt(pick, g_hi) + (_dot_nt(pick, g_mid) + _dot_nt(pick, g_lo))
            for hd in range(N_HEADS):
                cols = head_cols(hd)
                hgrn_items.append((hq[:, cols], g_all[:, cols], bc_all[:, cols], hi[:, cols], h_states[b][hd]))
                gdn_items.append((gq[:, cols], gk[:, cols], gv[:, cols], ab[:, N_HEADS + hd:N_HEADS + hd + 1],
                                  gcum[:, hd:hd + 1], gcum_rows[hd:hd + 1, :], g_states[b][hd]))
        hgrn_out = _hgrn_heads(hgrn_items)
        gdn_out = _gdn_heads(gdn_items)
        for b in range(n_batch):
            hg, gz = loaded[b][3], loaded[b][7]
            pieces = []
            for group, norm_w, gate in ((hgrn_out, hn, hg), (gdn_out, gn, gz)):
                for hd in range(N_HEADS):
                    o = group[b * N_HEADS + hd][0]
                    pieces.append((_rms_norm(o, norm_w) * gate[:, head_cols(hd)].astype(F32)).astype(BF16))
            out_ref[b, rows, :] = jnp.concatenate(pieces, axis=1)
            for hd in range(N_HEADS):
                hst_ref[b, hd] = hgrn_out[b * N_HEADS + hd][1]
                gst_ref[b, hd] = gdn_out[b * N_HEADS + hd][1]
        return carry

    lax.fori_loop(0, tb // c, chunk_body, 0)


def _mixer(hq, g, hi, hg, gq, gk, gv, gz, ab, hgrn_norm, gdn_norm, batch, seq_len, tb):
    n_tok = hq.shape[0]
    assert n_tok == batch * seq_len and seq_len % tb == 0 and tb % CHUNK == 0
    per_batch = lambda a: a.reshape(batch, seq_len, a.shape[-1])
    tok_spec = lambda width: pl.BlockSpec((batch, tb, width), lambda i: (0, i, 0))
    state = pltpu.VMEM((batch, N_HEADS, HEAD_DIM, HEAD_DIM), F32)
    mixed = pl.pallas_call(
        _mixer_kernel,
        grid=(seq_len // tb,),
        in_specs=[tok_spec(D_GROUP)] * 8 + [tok_spec(HEAD_DIM),
                                            _const_spec(hgrn_norm.shape), _const_spec(gdn_norm.shape)],
        out_specs=tok_spec(2 * D_GROUP),
        out_shape=jax.ShapeDtypeStruct((batch, seq_len, 2 * D_GROUP), BF16),
        scratch_shapes=[state, state],
        compiler_params=pltpu.CompilerParams(dimension_semantics=("arbitrary",),
                                             vmem_limit_bytes=V7X_VMEM_LIMIT_BYTES),
        name="mixer",
    )(*(per_batch(a) for a in (hq, g, hi, hg, gq, gk, gv, gz, ab)), hgrn_norm, gdn_norm)
    return mixed.reshape(n_tok, 2 * D_GROUP)


def _post_kernel(tiles_per_seq, x_ref, mix_ref, p_ref, wout_ref, wup_ref, cw_ref, wdown_ref, wple_ref,
                 wgate_ref, n_mix_ref, n_pre_ref, n_post_ref, n_ple_ref, out_ref,
                 ubuf_ref, halo_ref, acc_ref):
    tm = x_ref.shape[0]
    d_ff = wdown_ref.shape[0]
    i = pl.program_id(0)

    @pl.when(i % tiles_per_seq == 0)
    def _():
        halo_ref[...] = jnp.zeros(halo_ref.shape, F32)

    x1 = x_ref[...] + _rms_norm(_dot(mix_ref[...], wout_ref[...]), n_mix_ref[...])
    h2 = _rms_norm(x1, n_pre_ref[...]).astype(BF16)

    def conv(cols):
        width = cols.stop - cols.start
        u = _dot(h2, wup_ref[:, cols])
        ubuf_ref[0:HALO, 0:width] = halo_ref[:, cols]
        ubuf_ref[HALO:HALO + tm, 0:width] = u
        halo_ref[:, cols] = u[tm - HALO:tm, :]
        out = cw_ref[FFN_CONV - 1:FFN_CONV, cols] * u
        for k in range(FFN_CONV - 1):
            off = HALO - (FFN_CONV - 1) + k
            out = out + cw_ref[k:k + 1, cols] * ubuf_ref[off:off + tm, 0:width]
        return out

    for j in range(d_ff // FF_TILE):
        gate = conv(slice(j * FF_TILE, (j + 1) * FF_TILE))
        up = conv(slice(d_ff + j * FF_TILE, d_ff + (j + 1) * FF_TILE))
        act = (_silu(gate) * up).astype(BF16)
        part = _dot(act, wdown_ref[j * FF_TILE:(j + 1) * FF_TILE, :])
        if j == 0:
            acc_ref[...] = part
        else:
            acc_ref[...] += part

    x2 = x1 + _rms_norm(acc_ref[...], n_post_ref[...])
    ple = _dot(p_ref[...].astype(BF16), wple_ref[...]) * _sigmoid(_dot(x2.astype(BF16), wgate_ref[...]))
    out_ref[...] = x2 + _rms_norm(ple, n_ple_ref[...])


def _post(x2d, mixed, p2d, w_out, w_up, ffn_conv, w_down, w_ple, w_gate, n_mix, n_pre, n_post, n_ple,
          seq_len, tm):
    n_tok, d_model = x2d.shape
    d_ff = w_down.shape[0]
    assert seq_len % tm == 0 and d_ff % FF_TILE == 0
    row = lambda i: (i, 0)
    tok_spec = lambda width: pl.BlockSpec((tm, width), row)
    consts = (w_out, w_up, ffn_conv, w_down, w_ple, w_gate, n_mix, n_pre, n_post, n_ple)
    return pl.pallas_call(
        functools.partial(_post_kernel, seq_len // tm),
        grid=(n_tok // tm,),
        in_specs=[tok_spec(d_model), tok_spec(mixed.shape[1]), tok_spec(p2d.shape[1])]
                 + [_const_spec(a.shape) for a in consts],
        out_specs=tok_spec(d_model),
        out_shape=jax.ShapeDtypeStruct((n_tok, d_model), F32),
        scratch_shapes=[pltpu.VMEM((tm + HALO, FF_TILE), F32),
                        pltpu.VMEM((HALO, 2 * d_ff), F32),
                        pltpu.VMEM((tm, d_model), F32)],
        compiler_params=pltpu.CompilerParams(dimension_semantics=("arbitrary",),
                                             vmem_limit_bytes=V7X_VMEM_LIMIT_BYTES),
        name="post",
    )(x2d, mixed, p2d, *consts)


def _layer(x2d, p2d, mix_norm_pre, mix_norm_post, w_in, lb_logits, hgrn_out_norm, gdn_conv, gdn_a_log,
           gdn_dt_bias, gdn_out_norm, w_out, ffn_norm_pre, ffn_norm_post, w_up, ffn_conv, w_down, w_ple,
           w_ple_gate, ple_norm_post, batch, seq_len, in_tile, mix_tile, post_tile):
    n_main = 8 * D_GROUP
    n_ab = 2 * N_HEADS
    assert w_in.shape[1] == n_main + n_ab
    row = lambda v: v.reshape(1, -1).astype(F32)
    pad_lanes = lambda v: jnp.pad(row(v), ((0, 0), (0, HEAD_DIM - v.shape[-1])))
    w_main = w_in[:, :n_main].astype(BF16)
    w_ab = jnp.pad(w_in[:, n_main:], ((0, 0), (0, HEAD_DIM - n_ab))).astype(BF16)
    hq, g, hi, hg, gq, gk, gv, gz, ab = _in_proj(
        x2d, row(mix_norm_pre), w_main, w_ab, lb_logits.astype(F32), gdn_conv.astype(F32),
        pad_lanes(gdn_a_log), pad_lanes(gdn_dt_bias), seq_len, in_tile)
    mixed = _mixer(hq, g, hi, hg, gq, gk, gv, gz, ab, row(hgrn_out_norm), row(gdn_out_norm),
                   batch, seq_len, mix_tile)
    return _post(x2d, mixed, p2d, w_out.astype(BF16), w_up.astype(BF16), ffn_conv.astype(F32),
                 w_down.astype(BF16), w_ple.astype(BF16), w_ple_gate.astype(BF16),
                 row(mix_norm_post), row(ffn_norm_pre), row(ffn_norm_post), row(ple_norm_post),
                 seq_len, post_tile)


def kernel(x, p, mix_norm_pre, mix_norm_post, w_in, hgrn_lb_logits, hgrn_out_norm, gdn_conv, gdn_a_log,
           gdn_dt_bias, gdn_out_norm, w_out, ffn_norm_pre, ffn_norm_post, w_up, ffn_conv, w_down, w_ple,
           w_ple_gate, ple_norm_post):
    batch, seq_len, d_model = x.shape
    depth = p.shape[0]
    assert depth == 1 and hgrn_lb_logits.shape[0] == 2
    x2d = x.reshape(batch * seq_len, d_model)
    for i in range(depth):
        x2d = _layer(x2d, p[i].reshape(batch * seq_len, -1), mix_norm_pre[i], mix_norm_post[i], w_in[i],
                     hgrn_lb_logits, hgrn_out_norm[i], gdn_conv[i], gdn_a_log[i], gdn_dt_bias[i],
                     gdn_out_norm[i], w_out[i], ffn_norm_pre[i], ffn_norm_post[i], w_up[i], ffn_conv[i],
                     w_down[i], w_ple[i], w_ple_gate[i], ple_norm_post[i],
                     batch, seq_len, IN_TILE, MIX_TILE, POST_TILE)
    return x2d.reshape(batch, seq_len, d_model)
```

```python
import functools

import jax
import jax.numpy as jnp
from jax import lax
from jax.experimental import pallas as pl
from jax.experimental.pallas import tpu as pltpu

F32 = jnp.float32
BF16 = jnp.bfloat16

HEAD_DIM = 128
N_HEADS = 4
D_GROUP = N_HEADS * HEAD_DIM
GDN_CONV = 4
FFN_CONV = 3
CHUNK = 64
SUB = 8
EPS = 1e-6
NEG_LOG2_E = -1.4426950408889634
HALO = 8

V7X_VMEM_LIMIT_BYTES = 60000 * 1024

IN_TILE = 512
MIX_TILE = 256
POST_TILE = 512
FF_TILE = 256


def _dot(a, b):
    return jnp.dot(a, b, preferred_element_type=F32)


def _dot_nt(a, b):
    return lax.dot_general(a, b, (((1,), (1,)), ((), ())), preferred_element_type=F32)


def _dot_tn(a, b):
    return lax.dot_general(a, b, (((0,), (0,)), ((), ())), preferred_element_type=F32)


def _split3(x):
    hi = x.astype(BF16)
    r = x - hi.astype(F32)
    mid = r.astype(BF16)
    lo = (r - mid.astype(F32)).astype(BF16)
    return hi, mid, lo


def _sigmoid(x):
    return 1.0 / (1.0 + jnp.exp2(x * NEG_LOG2_E))


def _silu(x):
    return x * _sigmoid(x)


def _softplus(x):
    return jnp.maximum(x, 0.0) + jnp.log(1.0 + jnp.exp(-jnp.abs(x)))


def _rms_norm(x, w):
    return x * lax.rsqrt(jnp.mean(x * x, axis=-1, keepdims=True) + EPS) * w


def _iota2(shape, dim):
    return lax.broadcasted_iota(jnp.int32, shape, dim)


def _shift_rows(cur, prev, k):
    rolled = pltpu.roll(cur, k, 0)
    head = jnp.where(_iota2((HALO, 1), 0) < k, pltpu.roll(prev, k, 0), rolled[:HALO])
    return jnp.concatenate([head, rolled[HALO:]], axis=0)


def _const_spec(shape):
    nd = len(shape)
    return pl.BlockSpec(shape, lambda *_: (0,) * nd, pipeline_mode=pl.Buffered(1))


def _in_proj_kernel(tiles_per_seq, x_ref, nw_ref, w_ref, wab_ref, lbl_ref, cw_ref, alog_ref, dtb_ref,
                    hq_ref, g_ref, hi_ref, hg_ref, gq_ref, gk_ref, gv_ref, gz_ref, ab_ref,
                    halo_ref):
    tm = x_ref.shape[0]
    i = pl.program_id(0)

    @pl.when(i % tiles_per_seq == 0)
    def _():
        halo_ref[...] = jnp.zeros(halo_ref.shape, F32)

    h = _rms_norm(x_ref[...], nw_ref[...]).astype(BF16)

    def proj(group):
        return _dot(h, w_ref[:, group * D_GROUP:(group + 1) * D_GROUP])

    def silu_to(out_ref):
        def epilogue(y):
            out_ref[...] = _silu(y).astype(BF16)
        return epilogue

    def forget_gate(y):
        l0 = lbl_ref[0:1, :]
        l1 = lbl_ref[1:2, :]
        lmax = jnp.maximum(l0, l1)
        e0 = jnp.exp(l0 - lmax)
        e1 = jnp.exp(l1 - lmax)
        lb = e0 / (e0 + e1)
        g_ref[...] = jnp.log(lb + (1.0 - lb) * _sigmoid(y))

    def value(y):
        hi_ref[...] = y.astype(BF16)

    def short_conv(j, out_ref):
        def epilogue(pre):
            cols = slice(j * D_GROUP, (j + 1) * D_GROUP)
            prev = halo_ref[:, cols]
            halo_ref[:, cols] = pre[tm - HALO:tm, :]
            acc = cw_ref[GDN_CONV - 1:GDN_CONV, cols] * pre
            for k in range(GDN_CONV - 1):
                acc = acc + cw_ref[k:k + 1, cols] * _shift_rows(pre, prev, GDN_CONV - 1 - k)
            y = _silu(acc)
            if j == 2:
                out_ref[...] = y.astype(BF16)
                return
            scale = HEAD_DIM ** -0.5 if j == 0 else 1.0
            for hd in range(N_HEADS):
                yh = y[:, hd * HEAD_DIM:(hd + 1) * HEAD_DIM]
                nrm = lax.rsqrt(jnp.sum(yh * yh, axis=-1, keepdims=True) + EPS) * scale
                out_ref[:, hd * HEAD_DIM:(hd + 1) * HEAD_DIM] = (yh * nrm).astype(BF16)
        return epilogue

    def decay_beta(z):
        lane = lax.broadcasted_iota(jnp.int32, z.shape, 1)
        log_a = -jnp.exp(alog_ref[...]) * _softplus(z + dtb_ref[...])
        ab_ref[...] = jnp.where(lane < N_HEADS, log_a, _sigmoid(z))

    epilogues = [silu_to(hq_ref), forget_gate, value, silu_to(hg_ref), short_conv(0, gq_ref),
                 short_conv(1, gk_ref), short_conv(2, gv_ref), silu_to(gz_ref), decay_beta]
    for group, epilogue in enumerate(epilogues[:-1]):
        epilogue(proj(group))
    epilogues[-1](_dot(h, wab_ref[...]))


def _in_proj(x2d, norm_w, w_main, w_ab, lb_logits, conv_w, a_log_pad, dt_bias_pad, seq_len, tm):
    n_tok, d_model = x2d.shape
    assert seq_len % tm == 0 and n_tok % seq_len == 0
    grid = (n_tok // tm,)
    row = lambda i: (i, 0)
    tok_spec = lambda width: pl.BlockSpec((tm, width), row)
    out_shapes = [jax.ShapeDtypeStruct((n_tok, D_GROUP), BF16) for _ in range(8)]
    out_shapes[1] = jax.ShapeDtypeStruct((n_tok, D_GROUP), F32)
    out_shapes.append(jax.ShapeDtypeStruct((n_tok, HEAD_DIM), F32))
    return pl.pallas_call(
        functools.partial(_in_proj_kernel, seq_len // tm),
        grid=grid,
        in_specs=[tok_spec(d_model), _const_spec(norm_w.shape), _const_spec(w_main.shape),
                  _const_spec(w_ab.shape), _const_spec(lb_logits.shape), _const_spec(conv_w.shape),
                  _const_spec(a_log_pad.shape), _const_spec(dt_bias_pad.shape)],
        out_specs=[tok_spec(D_GROUP)] * 8 + [tok_spec(HEAD_DIM)],
        out_shape=out_shapes,
        scratch_shapes=[pltpu.VMEM((HALO, 3 * D_GROUP), F32)],
        compiler_params=pltpu.CompilerParams(dimension_semantics=("arbitrary",),
                                             vmem_limit_bytes=V7X_VMEM_LIMIT_BYTES),
        name="in_proj",
    )(x2d, norm_w, w_main, w_ab, lb_logits, conv_w, a_log_pad, dt_bias_pad)


def _hgrn_level_operands(q, kk, bc, m):
    c, d = q.shape
    row = _iota2((c, 1), 0)
    pieces = []
    for b in range(c // (2 * m)):
        r = b * 2 * m + m - 1
        pieces.append(jnp.broadcast_to(bc[r:r + 1, :], (2 * m, d)))
    ref = pieces[0] if len(pieces) == 1 else jnp.concatenate(pieces, axis=0)
    upper = (row & m) != 0
    diff = bc - ref
    e = jnp.exp(jnp.where(upper, diff, -diff))
    qt = jnp.where(upper, q * e, 0.0).astype(BF16)
    kt = jnp.where(upper, 0.0, kk * e).astype(BF16)
    return qt, kt


def _hgrn_diag_scores(q, kk, bc):
    c = q.shape[0]
    lane = _iota2((SUB, c), 1)
    tiles = []
    for r in range(c // SUB):
        blk_bc = bc[r * SUB:(r + 1) * SUB, :]
        blk_q = q[r * SUB:(r + 1) * SUB, :]
        tile = jnp.zeros((SUB, c), F32)
        for sl in range(SUB):
            s = r * SUB + sl
            dec = jnp.exp(jnp.minimum(blk_bc - bc[s:s + 1, :], 0.0))
            val = jnp.sum(blk_q * kk[s:s + 1, :] * dec, axis=-1, keepdims=True)
            tile = jnp.where(lane == s, val, tile)
        tiles.append(tile)
    return jnp.concatenate(tiles, axis=0)


def _hgrn_steps(items):
    c = items[0][0].shape[0]
    t_idx = _iota2((c, c), 0)
    s_idx = _iota2((c, c), 1)
    x_idx = t_idx ^ s_idx
    qs_bf, gs, bcs, vs, sts = zip(*items)
    qs = [q.astype(F32) for q in qs_bf]
    kks = [1.0 - jnp.exp(g) for g in gs]
    b_lasts = [bc[c - 1:c, :] for bc in bcs]
    q_decs = [(q * jnp.exp(bc)).astype(BF16) for q, bc in zip(qs, bcs)]
    k_decs = [(kk * jnp.exp(bl - bc)).astype(BF16) for kk, bl, bc in zip(kks, b_lasts, bcs)]
    yield
    inter = [_dot_nt(qd, st.astype(BF16)) for qd, st in zip(q_decs, sts)]
    new_sts = [st * jnp.exp(bl) + _dot_tn(v, kd) for st, bl, v, kd in zip(sts, b_lasts, vs, k_decs)]
    yield
    scores = []
    for q, kk, bc in zip(qs, kks, bcs):
        scores.append(jnp.where((t_idx >= s_idx) & (x_idx < SUB), _hgrn_diag_scores(q, kk, bc), 0.0))
        yield
    m = c // 2
    while m >= SUB:
        operands = [_hgrn_level_operands(q, kk, bc, m) for q, kk, bc in zip(qs, kks, bcs)]
        prods = [_dot_nt(qt, kt) for qt, kt in operands]
        yield
        scores = [sc + jnp.where(x_idx < 2 * m, p, 0.0) for sc, p in zip(scores, prods)]
        m //= 2
    outs = [o + _dot(sc.astype(BF16), v) for o, sc, v in zip(inter, scores, vs)]
    return list(zip(outs, new_sts))


def _unit_lower_inverse_steps(mats):
    c = mats[0].shape[0]
    t_idx = _iota2((c, c), 0)
    s_idx = _iota2((c, c), 1)
    x_idx = t_idx ^ s_idx
    eye = (t_idx == s_idx).astype(F32)
    d1 = [jnp.where(x_idx < SUB, a, 0.0) for a in mats]
    d1_bf = [x.astype(BF16) for x in d1]
    d2_bf = [_dot(x, x).astype(BF16) for x in d1_bf]
    yield
    d4_bf = [_dot(x, x).astype(BF16) for x in d2_bf]
    ts = [eye - x for x in d1]
    ts = [t + _dot(t.astype(BF16), x) for t, x in zip(ts, d2_bf)]
    yield
    ts = [t + _dot(t.astype(BF16), x) for t, x in zip(ts, d4_bf)]
    yield
    m = SUB
    while m < c:
        cms_bf = [jnp.where((x_idx < 2 * m) & (x_idx >= m), a, 0.0).astype(BF16) for a in mats]
        ts_bf = [t.astype(BF16) for t in ts]
        inner_bf = [_dot(cm, t).astype(BF16) for cm, t in zip(cms_bf, ts_bf)]
        yield
        ts = [t - _dot(tb, x) for t, tb, x in zip(ts, ts_bf, inner_bf)]
        yield
        m *= 2
    return ts


def _gdn_steps(items):
    c, d = items[0][0].shape
    t_idx = _iota2((c, c), 0)
    s_idx = _iota2((c, c), 1)
    n = len(items)
    qs, ks_bf, vs, betas, gcs, gc_rows, states = zip(*items)
    ks = [k.astype(F32) for k in ks_bf]
    kbs = [k * b for k, b in zip(ks, betas)]
    decays = [jnp.exp(jnp.minimum(gc - gr, 0.0)) for gc, gr in zip(gcs, gc_rows)]
    boths = [_dot_nt(jnp.concatenate([kb.astype(BF16), q], axis=0), k) for kb, q, k in zip(kbs, qs, ks_bf)]
    yield
    a_mats = [jnp.where(t_idx > s_idx, bo[:c] * dc, 0.0) for bo, dc in zip(boths, decays)]
    attns = [jnp.where(t_idx >= s_idx, bo[c:] * dc, 0.0).astype(BF16) for bo, dc in zip(boths, decays)]
    ts = yield from _unit_lower_inverse_steps(a_mats)
    e_gcs = [jnp.exp(gc) for gc in gcs]
    rhss = [jnp.concatenate([v.astype(F32) * b, kb * e], axis=1).astype(BF16)
            for v, b, kb, e in zip(vs, betas, kbs, e_gcs)]
    uws = [_dot(t.astype(BF16), r) for t, r in zip(ts, rhss)]
    yield
    q_decs = [(q.astype(F32) * e).astype(BF16) for q, e in zip(qs, e_gcs)]
    ws_qs = [_dot(jnp.concatenate([uw[:, d:].astype(BF16), qd], axis=0), s.astype(BF16))
             for uw, qd, s in zip(uws, q_decs, states)]
    yield
    v_news = [(uw[:, :d] - wq[:c]).astype(BF16) for uw, wq in zip(uws, ws_qs)]
    outs = [wq[c:] + _dot(at, vn) for wq, at, vn in zip(ws_qs, attns, v_news)]
    g_lasts = [gc[c - 1:c, :] for gc in gcs]
    k_decs = [(k * jnp.exp(gl - gc)).astype(BF16) for k, gl, gc in zip(ks, g_lasts, gcs)]
    new_states = [s * jnp.exp(gl) + _dot_tn(kd, vn) for s, gl, kd, vn in zip(states, g_lasts, k_decs, v_news)]
    return [(outs[i], new_states[i]) for i in range(n)]


def _run_interleaved(*step_generators):
    results = [None] * len(step_generators)
    alive = list(range(len(step_generators)))
    while alive:
        for i in list(alive):
            try:
                next(step_generators[i])
            except StopIteration as stop:
                results[i] = stop.value
                alive.remove(i)
    return results


def _mixer_kernel(hq_ref, g_ref, hi_ref, hg_ref, gq_ref, gk_ref, gv_ref, gz_ref, ab_ref, hn_ref, gn_ref,
                  out_ref, hst_ref, gst_ref):
    n_batch, tb, _ = hq_ref.shape
    c = CHUNK

    @pl.when(pl.program_id(0) == 0)
    def _():
        hst_ref[...] = jnp.zeros(hst_ref.shape, F32)
        gst_ref[...] = jnp.zeros(gst_ref.shape, F32)

    tri = (_iota2((c, c), 0) >= _iota2((c, c), 1)).astype(BF16)
    pick = (_iota2((SUB, HEAD_DIM), 0) == _iota2((SUB, HEAD_DIM), 1)).astype(BF16)

    def cumsum_rows(x):
        hi, mid, lo = _split3(x)
        return _dot(tri, hi) + (_dot(tri, mid) + _dot(tri, lo))

    def head_cols(hd):
        return slice(hd * HEAD_DIM, (hd + 1) * HEAD_DIM)

    def chunk_body(ci, carry):
        rows = pl.ds(pl.multiple_of(ci * c, c), c)
        loaded = [[r[b, rows, :] for r in (hq_ref, g_ref, hi_ref, hg_ref, gq_ref, gk_ref, gv_ref, gz_ref, ab_ref)]
                  for b in range(n_batch)]
        h_states = [[hst_ref[b, hd] for hd in range(N_HEADS)] for b in range(n_batch)]
        g_states = [[gst_ref[b, hd] for hd in range(N_HEADS)] for b in range(n_batch)]
        hn = hn_ref[...]
        gn = gn_ref[...]
        hgrn_items, gdn_items = [], []
        for b in range(n_batch):
            hq, g_all, hi, hg, gq, gk, gv, gz, ab = loaded[b]
            bc_all = cumsum_rows(g_all)
            gcum = cumsum_rows(ab)
            g_hi, g_mid, g_lo = _split3(gcum)
            gcum_rows = _dot_nt(pick, g_hi) + (_dot_nt(pick, g_mid) + _dot_nt(pick, g_lo))
            for hd in range(N_HEADS):
                cols = head_cols(hd)
                hgrn_items.append((hq[:, cols], g_all[:, cols], bc_all[:, cols], hi[:, cols], h_states[b][hd]))
                gdn_items.append((gq[:, cols], gk[:, cols], gv[:, cols], ab[:, N_HEADS + hd:N_HEADS + hd + 1],
                                  gcum[:, hd:hd + 1], gcum_rows[hd:hd + 1, :], g_states[b][hd]))
        gdn_out, hgrn_out = _run_interleaved(_gdn_steps(gdn_items), _hgrn_steps(hgrn_items))
        for b in range(n_batch):
            hg, gz = loaded[b][3], loaded[b][7]
            pieces = []
            for group, norm_w, gate in ((hgrn_out, hn, hg), (gdn_out, gn, gz)):
                for hd in range(N_HEADS):
                    o = group[b * N_HEADS + hd][0]
                    pieces.append((_rms_norm(o, norm_w) * gate[:, head_cols(hd)].astype(F32)).astype(BF16))
            out_ref[b, rows, :] = jnp.concatenate(pieces, axis=1)
            for hd in range(N_HEADS):
                hst_ref[b, hd] = hgrn_out[b * N_HEADS + hd][1]
                gst_ref[b, hd] = gdn_out[b * N_HEADS + hd][1]
        return carry

    lax.fori_loop(0, tb // c, chunk_body, 0)


def _mixer(hq, g, hi, hg, gq, gk, gv, gz, ab, hgrn_norm, gdn_norm, batch, seq_len, tb):
    n_tok = hq.shape[0]
    assert n_tok == batch * seq_len and seq_len % tb == 0 and tb % CHUNK == 0
    per_batch = lambda a: a.reshape(batch, seq_len, a.shape[-1])
    tok_spec = lambda width: pl.BlockSpec((batch, tb, width), lambda i: (0, i, 0))
    state = pltpu.VMEM((batch, N_HEADS, HEAD_DIM, HEAD_DIM), F32)
    mixed = pl.pallas_call(
        _mixer_kernel,
        grid=(seq_len // tb,),
        in_specs=[tok_spec(D_GROUP)] * 8 + [tok_spec(HEAD_DIM),
                                            _const_spec(hgrn_norm.shape), _const_spec(gdn_norm.shape)],
        out_specs=tok_spec(2 * D_GROUP),
        out_shape=jax.ShapeDtypeStruct((batch, seq_len, 2 * D_GROUP), BF16),
        scratch_shapes=[state, state],
        compiler_params=pltpu.CompilerParams(dimension_semantics=("arbitrary",),
                                             vmem_limit_bytes=V7X_VMEM_LIMIT_BYTES),
        name="mixer",
    )(*(per_batch(a) for a in (hq, g, hi, hg, gq, gk, gv, gz, ab)), hgrn_norm, gdn_norm)
    return mixed.reshape(n_tok, 2 * D_GROUP)


def _post_kernel(tiles_per_seq, x_ref, mix_ref, p_ref, wout_ref, wup_ref, cw_ref, wdown_ref, wple_ref,
                 wgate_ref, n_mix_ref, n_pre_ref, n_post_ref, n_ple_ref, out_ref,
                 halo_ref, act_ref):
    tm = x_ref.shape[0]
    d_ff = wdown_ref.shape[0]
    i = pl.program_id(0)

    @pl.when(i % tiles_per_seq == 0)
    def _():
        halo_ref[...] = jnp.zeros(halo_ref.shape, F32)

    x1 = x_ref[...] + _rms_norm(_dot(mix_ref[...], wout_ref[...]), n_mix_ref[...])
    h2 = _rms_norm(x1, n_pre_ref[...]).astype(BF16)

    def conv(cols):
        u = _dot(h2, wup_ref[:, cols])
        prev = halo_ref[:, cols]
        halo_ref[:, cols] = u[tm - HALO:tm, :]
        out = cw_ref[FFN_CONV - 1:FFN_CONV, cols] * u
        for k in range(FFN_CONV - 1):
            out = out + cw_ref[k:k + 1, cols] * _shift_rows(u, prev, FFN_CONV - 1 - k)
        return out

    for j in range(d_ff // FF_TILE):
        gate = conv(slice(j * FF_TILE, (j + 1) * FF_TILE))
        up = conv(slice(d_ff + j * FF_TILE, d_ff + (j + 1) * FF_TILE))
        act_ref[:, j * FF_TILE:(j + 1) * FF_TILE] = (_silu(gate) * up).astype(BF16)

    x2 = x1 + _rms_norm(_dot(act_ref[...], wdown_ref[...]), n_post_ref[...])
    ple = _dot(p_ref[...].astype(BF16), wple_ref[...]) * _sigmoid(_dot(x2.astype(BF16), wgate_ref[...]))
    out_ref[...] = x2 + _rms_norm(ple, n_ple_ref[...])


def _post(x2d, mixed, p2d, w_out, w_up, ffn_conv, w_down, w_ple, w_gate, n_mix, n_pre, n_post, n_ple,
          seq_len, tm):
    n_tok, d_model = x2d.shape
    d_ff = w_down.shape[0]
    assert seq_len % tm == 0 and d_ff % FF_TILE == 0
    row = lambda i: (i, 0)
    tok_spec = lambda width: pl.BlockSpec((tm, width), row)
    consts = (w_out, w_up, ffn_conv, w_down, w_ple, w_gate, n_mix, n_pre, n_post, n_ple)
    return pl.pallas_call(
        functools.partial(_post_kernel, seq_len // tm),
        grid=(n_tok // tm,),
        in_specs=[tok_spec(d_model), tok_spec(mixed.shape[1]), tok_spec(p2d.shape[1])]
                 + [_const_spec(a.shape) for a in consts],
        out_specs=tok_spec(d_model),
        out_shape=jax.ShapeDtypeStruct((n_tok, d_model), F32),
        scratch_shapes=[pltpu.VMEM((HALO, 2 * d_ff), F32),
                        pltpu.VMEM((tm, d_ff), BF16)],
        compiler_params=pltpu.CompilerParams(dimension_semantics=("arbitrary",),
                                             vmem_limit_bytes=V7X_VMEM_LIMIT_BYTES),
        name="post",
    )(x2d, mixed, p2d, *consts)


def _layer(x2d, p2d, mix_norm_pre, mix_norm_post, w_in, lb_logits, hgrn_out_norm, gdn_conv, gdn_a_log,
           gdn_dt_bias, gdn_out_norm, w_out, ffn_norm_pre, ffn_norm_post, w_up, ffn_conv, w_down, w_ple,
           w_ple_gate, ple_norm_post, batch, seq_len, in_tile, mix_tile, post_tile):
    n_main = 8 * D_GROUP
    n_ab = 2 * N_HEADS
    assert w_in.shape[1] == n_main + n_ab
    row = lambda v: v.reshape(1, -1).astype(F32)
    pad_lanes = lambda v: jnp.pad(row(v), ((0, 0), (0, HEAD_DIM - v.shape[-1])))
    w_main = w_in[:, :n_main].astype(BF16)
    w_ab = jnp.pad(w_in[:, n_main:], ((0, 0), (0, HEAD_DIM - n_ab))).astype(BF16)
    hq, g, hi, hg, gq, gk, gv, gz, ab = _in_proj(
        x2d, row(mix_norm_pre), w_main, w_ab, lb_logits.astype(F32), gdn_conv.astype(F32),
        pad_lanes(gdn_a_log), pad_lanes(gdn_dt_bias), seq_len, in_tile)
    mixed = _mixer(hq, g, hi, hg, gq, gk, gv, gz, ab, row(hgrn_out_norm), row(gdn_out_norm),
                   batch, seq_len, mix_tile)
    return _post(x2d, mixed, p2d, w_out.astype(BF16), w_up.astype(BF16), ffn_conv.astype(F32),
                 w_down.astype(BF16), w_ple.astype(BF16), w_ple_gate.astype(BF16),
                 row(mix_norm_post), row(ffn_norm_pre), row(ffn_norm_post), row(ple_norm_post),
                 seq_len, post_tile)


def kernel(x, p, mix_norm_pre, mix_norm_post, w_in, hgrn_lb_logits, hgrn_out_norm, gdn_conv, gdn_a_log,
           gdn_dt_bias, gdn_out_norm, w_out, ffn_norm_pre, ffn_norm_post, w_up, ffn_conv, w_down, w_ple,
           w_ple_gate, ple_norm_post):
    batch, seq_len, d_model = x.shape
    depth = p.shape[0]
    assert depth == 1 and hgrn_lb_logits.shape[0] == 2
    x2d = x.reshape(batch * seq_len, d_model)
    for i in range(depth):
        x2d = _layer(x2d, p[i].reshape(batch * seq_len, -1), mix_norm_pre[i], mix_norm_post[i], w_in[i],
                     hgrn_lb_logits, hgrn_out_norm[i], gdn_conv[i], gdn_a_log[i], gdn_dt_bias[i],
                     gdn_out_norm[i], w_out[i], ffn_norm_pre[i], ffn_norm_post[i], w_up[i], ffn_conv[i],
                     w_down[i], w_ple[i], w_ple_gate[i], ple_norm_post[i],
                     batch, seq_len, IN_TILE, MIX_TILE, POST_TILE)
    return x2d.reshape(batch, seq_len, d_model)
```

```python
import functools

import jax
import jax.numpy as jnp
from jax import lax
from jax.experimental import pallas as pl
from jax.experimental.pallas import tpu as pltpu

F32 = jnp.float32
BF16 = jnp.bfloat16

HEAD_DIM = 128
N_HEADS = 4
D_GROUP = N_HEADS * HEAD_DIM
GDN_CONV = 4
FFN_CONV = 3
CHUNK = 64
SUB = 8
EPS = 1e-6
NEG_LOG2_E = -1.4426950408889634
HALO = 8

V7X_VMEM_LIMIT_BYTES = 60000 * 1024

IN_TILE = 512
MIX_TILE = 256
FF_TILE = 256
MIXER_STEPS_PER_POST_STEP = 3


def _dot(a, b):
    return jnp.dot(a, b, preferred_element_type=F32)


def _dot_nt(a, b):
    return lax.dot_general(a, b, (((1,), (1,)), ((), ())), preferred_element_type=F32)


def _dot_tn(a, b):
    return lax.dot_general(a, b, (((0,), (0,)), ((), ())), preferred_element_type=F32)


def _split3(x):
    hi = x.astype(BF16)
    r = x - hi.astype(F32)
    mid = r.astype(BF16)
    lo = (r - mid.astype(F32)).astype(BF16)
    return hi, mid, lo


def _sigmoid(x):
    return 1.0 / (1.0 + jnp.exp2(x * NEG_LOG2_E))


def _silu(x):
    return x * _sigmoid(x)


def _softplus(x):
    return jnp.maximum(x, 0.0) + jnp.log(1.0 + jnp.exp(-jnp.abs(x)))


def _rms_norm(x, w):
    return x * lax.rsqrt(jnp.mean(x * x, axis=-1, keepdims=True) + EPS) * w


def _iota2(shape, dim):
    return lax.broadcasted_iota(jnp.int32, shape, dim)


def _shift_rows(cur, prev, k):
    rolled = pltpu.roll(cur, k, 0)
    head = jnp.where(_iota2((HALO, 1), 0) < k, pltpu.roll(prev, k, 0), rolled[:HALO])
    return jnp.concatenate([head, rolled[HALO:]], axis=0)


def _const_spec(shape):
    nd = len(shape)
    return pl.BlockSpec(shape, lambda *_: (0,) * nd, pipeline_mode=pl.Buffered(1))


def _in_proj_kernel(tiles_per_seq, x_ref, nw_ref, w_ref, wab_ref, lbl_ref, cw_ref, alog_ref, dtb_ref,
                    hq_ref, g_ref, hi_ref, hg_ref, gq_ref, gk_ref, gv_ref, gz_ref, ab_ref,
                    halo_ref):
    tm = x_ref.shape[0]
    i = pl.program_id(0)

    @pl.when(i % tiles_per_seq == 0)
    def _():
        halo_ref[...] = jnp.zeros(halo_ref.shape, F32)

    h = _rms_norm(x_ref[...], nw_ref[...]).astype(BF16)

    def proj(group):
        return _dot(h, w_ref[:, group * D_GROUP:(group + 1) * D_GROUP])

    def silu_to(out_ref):
        def epilogue(y):
            out_ref[...] = _silu(y).astype(BF16)
        return epilogue

    def forget_gate(y):
        l0 = lbl_ref[0:1, :]
        l1 = lbl_ref[1:2, :]
        lmax = jnp.maximum(l0, l1)
        e0 = jnp.exp(l0 - lmax)
        e1 = jnp.exp(l1 - lmax)
        lb = e0 / (e0 + e1)
        g_ref[...] = jnp.log(lb + (1.0 - lb) * _sigmoid(y))

    def value(y):
        hi_ref[...] = y.astype(BF16)

    def short_conv(j, out_ref):
        def epilogue(pre):
            cols = slice(j * D_GROUP, (j + 1) * D_GROUP)
            prev = halo_ref[:, cols]
            halo_ref[:, cols] = pre[tm - HALO:tm, :]
            acc = cw_ref[GDN_CONV - 1:GDN_CONV, cols] * pre
            for k in range(GDN_CONV - 1):
                acc = acc + cw_ref[k:k + 1, cols] * _shift_rows(pre, prev, GDN_CONV - 1 - k)
            y = _silu(acc)
            if j == 2:
                out_ref[...] = y.astype(BF16)
                return
            scale = HEAD_DIM ** -0.5 if j == 0 else 1.0
            for hd in range(N_HEADS):
                yh = y[:, hd * HEAD_DIM:(hd + 1) * HEAD_DIM]
                nrm = lax.rsqrt(jnp.sum(yh * yh, axis=-1, keepdims=True) + EPS) * scale
                out_ref[:, hd * HEAD_DIM:(hd + 1) * HEAD_DIM] = (yh * nrm).astype(BF16)
        return epilogue

    def decay_beta(z):
        lane = lax.broadcasted_iota(jnp.int32, z.shape, 1)
        log_a = -jnp.exp(alog_ref[...]) * _softplus(z + dtb_ref[...])
        ab_ref[...] = jnp.where(lane < N_HEADS, log_a, _sigmoid(z))

    epilogues = [silu_to(hq_ref), forget_gate, value, silu_to(hg_ref), short_conv(0, gq_ref),
                 short_conv(1, gk_ref), short_conv(2, gv_ref), silu_to(gz_ref), decay_beta]
    for group, epilogue in enumerate(epilogues[:-1]):
        epilogue(proj(group))
    epilogues[-1](_dot(h, wab_ref[...]))


def _in_proj(x2d, norm_w, w_main, w_ab, lb_logits, conv_w, a_log_pad, dt_bias_pad, seq_len, tm):
    n_tok, d_model = x2d.shape
    assert seq_len % tm == 0 and n_tok % seq_len == 0
    grid = (n_tok // tm,)
    row = lambda i: (i, 0)
    tok_spec = lambda width: pl.BlockSpec((tm, width), row)
    out_shapes = [jax.ShapeDtypeStruct((n_tok, D_GROUP), BF16) for _ in range(8)]
    out_shapes[1] = jax.ShapeDtypeStruct((n_tok, D_GROUP), F32)
    out_shapes.append(jax.ShapeDtypeStruct((n_tok, HEAD_DIM), F32))
    return pl.pallas_call(
        functools.partial(_in_proj_kernel, seq_len // tm),
        grid=grid,
        in_specs=[tok_spec(d_model), _const_spec(norm_w.shape), _const_spec(w_main.shape),
                  _const_spec(w_ab.shape), _const_spec(lb_logits.shape), _const_spec(conv_w.shape),
                  _const_spec(a_log_pad.shape), _const_spec(dt_bias_pad.shape)],
        out_specs=[tok_spec(D_GROUP)] * 8 + [tok_spec(HEAD_DIM)],
        out_shape=out_shapes,
        scratch_shapes=[pltpu.VMEM((HALO, 3 * D_GROUP), F32)],
        compiler_params=pltpu.CompilerParams(dimension_semantics=("arbitrary",),
                                             vmem_limit_bytes=V7X_VMEM_LIMIT_BYTES),
        name="in_proj",
    )(x2d, norm_w, w_main, w_ab, lb_logits, conv_w, a_log_pad, dt_bias_pad)


def _hgrn_level_operands(q, kk, bc, m):
    c, d = q.shape
    row = _iota2((c, 1), 0)
    pieces = []
    for b in range(c // (2 * m)):
        r = b * 2 * m + m - 1
        pieces.append(jnp.broadcast_to(bc[r:r + 1, :], (2 * m, d)))
    ref = pieces[0] if len(pieces) == 1 else jnp.concatenate(pieces, axis=0)
    upper = (row & m) != 0
    diff = bc - ref
    e = jnp.exp(jnp.where(upper, diff, -diff))
    qt = jnp.where(upper, q * e, 0.0).astype(BF16)
    kt = jnp.where(upper, 0.0, kk * e).astype(BF16)
    return qt, kt


def _hgrn_diag_scores(q, kk, bc):
    c = q.shape[0]
    lane = _iota2((SUB, c), 1)
    tiles = []
    for r in range(c // SUB):
        blk_bc = bc[r * SUB:(r + 1) * SUB, :]
        blk_q = q[r * SUB:(r + 1) * SUB, :]
        tile = jnp.zeros((SUB, c), F32)
        for sl in range(SUB):
            s = r * SUB + sl
            dec = jnp.exp(jnp.minimum(blk_bc - bc[s:s + 1, :], 0.0))
            val = jnp.sum(blk_q * kk[s:s + 1, :] * dec, axis=-1, keepdims=True)
            tile = jnp.where(lane == s, val, tile)
        tiles.append(tile)
    return jnp.concatenate(tiles, axis=0)


def _hgrn_steps(items):
    c = items[0][0].shape[0]
    t_idx = _iota2((c, c), 0)
    s_idx = _iota2((c, c), 1)
    x_idx = t_idx ^ s_idx
    qs_bf, gs, bcs, vs, sts = zip(*items)
    qs = [q.astype(F32) for q in qs_bf]
    kks = [1.0 - jnp.exp(g) for g in gs]
    b_lasts = [bc[c - 1:c, :] for bc in bcs]
    q_decs = [(q * jnp.exp(bc)).astype(BF16) for q, bc in zip(qs, bcs)]
    k_decs = [(kk * jnp.exp(bl - bc)).astype(BF16) for kk, bl, bc in zip(kks, b_lasts, bcs)]
    yield
    inter = [_dot_nt(qd, st.astype(BF16)) for qd, st in zip(q_decs, sts)]
    new_sts = [st * jnp.exp(bl) + _dot_tn(v, kd) for st, bl, v, kd in zip(sts, b_lasts, vs, k_decs)]
    yield
    scores = []
    for q, kk, bc in zip(qs, kks, bcs):
        scores.append(jnp.where((t_idx >= s_idx) & (x_idx < SUB), _hgrn_diag_scores(q, kk, bc), 0.0))
        yield
    m = c // 2
    while m >= SUB:
        operands = [_hgrn_level_operands(q, kk, bc, m) for q, kk, bc in zip(qs, kks, bcs)]
        prods = [_dot_nt(qt, kt) for qt, kt in operands]
        yield
        scores = [sc + jnp.where(x_idx < 2 * m, p, 0.0) for sc, p in zip(scores, prods)]
        m //= 2
    outs = [o + _dot(sc.astype(BF16), v) for o, sc, v in zip(inter, scores, vs)]
    return list(zip(outs, new_sts))


def _unit_lower_inverse_steps(mats):
    c = mats[0].shape[0]
    t_idx = _iota2((c, c), 0)
    s_idx = _iota2((c, c), 1)
    x_idx = t_idx ^ s_idx
    eye = (t_idx == s_idx).astype(F32)
    d1 = [jnp.where(x_idx < SUB, a, 0.0) for a in mats]
    d1_bf = [x.astype(BF16) for x in d1]
    d2_bf = [_dot(x, x).astype(BF16) for x in d1_bf]
    yield
    d4_bf = [_dot(x, x).astype(BF16) for x in d2_bf]
    ts = [eye - x for x in d1]
    ts = [t + _dot(t.astype(BF16), x) for t, x in zip(ts, d2_bf)]
    yield
    ts = [t + _dot(t.astype(BF16), x) for t, x in zip(ts, d4_bf)]
    yield
    m = SUB
    while m < c:
        cms_bf = [jnp.where((x_idx < 2 * m) & (x_idx >= m), a, 0.0).astype(BF16) for a in mats]
        ts_bf = [t.astype(BF16) for t in ts]
        inner_bf = [_dot(cm, t).astype(BF16) for cm, t in zip(cms_bf, ts_bf)]
        yield
        ts = [t - _dot(tb, x) for t, tb, x in zip(ts, ts_bf, inner_bf)]
        yield
        m *= 2
    return ts


def _gdn_steps(items):
    c, d = items[0][0].shape
    t_idx = _iota2((c, c), 0)
    s_idx = _iota2((c, c), 1)
    n = len(items)
    qs, ks_bf, vs, betas, gcs, gc_rows, states = zip(*items)
    ks = [k.astype(F32) for k in ks_bf]
    kbs = [k * b for k, b in zip(ks, betas)]
    decays = [jnp.exp(jnp.minimum(gc - gr, 0.0)) for gc, gr in zip(gcs, gc_rows)]
    boths = [_dot_nt(jnp.concatenate([kb.astype(BF16), q], axis=0), k) for kb, q, k in zip(kbs, qs, ks_bf)]
    yield
    a_mats = [jnp.where(t_idx > s_idx, bo[:c] * dc, 0.0) for bo, dc in zip(boths, decays)]
    attns = [jnp.where(t_idx >= s_idx, bo[c:] * dc, 0.0).astype(BF16) for bo, dc in zip(boths, decays)]
    ts = yield from _unit_lower_inverse_steps(a_mats)
    e_gcs = [jnp.exp(gc) for gc in gcs]
    rhss = [jnp.concatenate([v.astype(F32) * b, kb * e], axis=1).astype(BF16)
            for v, b, kb, e in zip(vs, betas, kbs, e_gcs)]
    uws = [_dot(t.astype(BF16), r) for t, r in zip(ts, rhss)]
    yield
    q_decs = [(q.astype(F32) * e).astype(BF16) for q, e in zip(qs, e_gcs)]
    ws_qs = [_dot(jnp.concatenate([uw[:, d:].astype(BF16), qd], axis=0), s.astype(BF16))
             for uw, qd, s in zip(uws, q_decs, states)]
    yield
    v_news = [(uw[:, :d] - wq[:c]).astype(BF16) for uw, wq in zip(uws, ws_qs)]
    outs = [wq[c:] + _dot(at, vn) for wq, at, vn in zip(ws_qs, attns, v_news)]
    g_lasts = [gc[c - 1:c, :] for gc in gcs]
    k_decs = [(k * jnp.exp(gl - gc)).astype(BF16) for k, gl, gc in zip(ks, g_lasts, gcs)]
    new_states = [s * jnp.exp(gl) + _dot_tn(kd, vn) for s, gl, kd, vn in zip(states, g_lasts, k_decs, v_news)]
    return [(outs[i], new_states[i]) for i in range(n)]


def _round_robin(step_generators):
    results = [None] * len(step_generators)
    alive = list(range(len(step_generators)))
    while alive:
        for i in list(alive):
            try:
                next(step_generators[i])
            except StopIteration as stop:
                results[i] = stop.value
                alive.remove(i)
        yield
    return results


def _run_weighted(primary, secondary, ratio):
    live = [True, True]

    def advance(gen, which):
        if live[which]:
            try:
                next(gen)
            except StopIteration:
                live[which] = False

    while live[0] or live[1]:
        for _ in range(ratio):
            advance(primary, 0)
        advance(secondary, 1)


def _mixer_block_steps(in_refs, hn_ref, gn_ref, hst_ref, gst_ref, mix_ref):
    n_batch, tb, _ = in_refs[0].shape
    c = CHUNK
    tri = (_iota2((c, c), 0) >= _iota2((c, c), 1)).astype(BF16)
    pick = (_iota2((SUB, HEAD_DIM), 0) == _iota2((SUB, HEAD_DIM), 1)).astype(BF16)

    def cumsum_rows(x):
        hi, mid, lo = _split3(x)
        return _dot(tri, hi) + (_dot(tri, mid) + _dot(tri, lo))

    def head_cols(hd):
        return slice(hd * HEAD_DIM, (hd + 1) * HEAD_DIM)

    h_states = [hst_ref[b, hd] for b in range(n_batch) for hd in range(N_HEADS)]
    g_states = [gst_ref[b, hd] for b in range(n_batch) for hd in range(N_HEADS)]
    hn = hn_ref[...]
    gn = gn_ref[...]
    mixed_rows = [[] for _ in range(n_batch)]
    for ci in range(tb // c):
        rows = slice(ci * c, (ci + 1) * c)
        loaded = [[r[b, rows, :] for r in in_refs] for b in range(n_batch)]
        hgrn_items, gdn_items = [], []
        for b in range(n_batch):
            hq, g_all, hi, hg, gq, gk, gv, gz, ab = loaded[b]
            bc_all = cumsum_rows(g_all)
            gcum = cumsum_rows(ab)
            g_hi, g_mid, g_lo = _split3(gcum)
            gcum_rows = _dot_nt(pick, g_hi) + (_dot_nt(pick, g_mid) + _dot_nt(pick, g_lo))
            for hd in range(N_HEADS):
                cols = head_cols(hd)
                hgrn_items.append((hq[:, cols], g_all[:, cols], bc_all[:, cols], hi[:, cols],
                                   h_states[b * N_HEADS + hd]))
                gdn_items.append((gq[:, cols], gk[:, cols], gv[:, cols], ab[:, N_HEADS + hd:N_HEADS + hd + 1],
                                  gcum[:, hd:hd + 1], gcum_rows[hd:hd + 1, :], g_states[b * N_HEADS + hd]))
        gdn_out, hgrn_out = yield from _round_robin([_gdn_steps(gdn_items), _hgrn_steps(hgrn_items)])
        h_states = [st for _, st in hgrn_out]
        g_states = [st for _, st in gdn_out]
        for b in range(n_batch):
            hg, gz = loaded[b][3], loaded[b][7]
            pieces = []
            for group, norm_w, gate in ((hgrn_out, hn, hg), (gdn_out, gn, gz)):
                for hd in range(N_HEADS):
                    o = group[b * N_HEADS + hd][0]
                    pieces.append((_rms_norm(o, norm_w) * gate[:, head_cols(hd)].astype(F32)).astype(BF16))
            mixed_rows[b].append(jnp.concatenate(pieces, axis=1))
        yield
    for b in range(n_batch):
        mix_ref[b] = jnp.concatenate(mixed_rows[b], axis=0)
        for hd in range(N_HEADS):
            hst_ref[b, hd] = h_states[b * N_HEADS + hd]
            gst_ref[b, hd] = g_states[b * N_HEADS + hd]


def _post_steps(x_ref, p_ref, mix_ref, wout_ref, wup_ref, cw_ref, wdown_ref, wple_ref, wgate_ref,
                n_mix_ref, n_pre_ref, n_post_ref, n_ple_ref, out_ref, halo_ref, act_ref):
    n_batch, tb, d_model = x_ref.shape
    d_ff = wdown_ref.shape[0]
    rows = n_batch * tb
    mix = mix_ref[...].reshape(rows, mix_ref.shape[-1])
    x1 = x_ref[...].reshape(rows, d_model) + _rms_norm(_dot(mix, wout_ref[...]), n_mix_ref[...])
    h2 = _rms_norm(x1, n_pre_ref[...]).astype(BF16)
    yield

    def conv(cols):
        u = _dot(h2, wup_ref[:, cols])
        outs = []
        for b in range(n_batch):
            ub = u[b * tb:(b + 1) * tb, :]
            prev = halo_ref[b, :, cols]
            halo_ref[b, :, cols] = ub[tb - HALO:tb, :]
            out = cw_ref[FFN_CONV - 1:FFN_CONV, cols] * ub
            for k in range(FFN_CONV - 1):
                out = out + cw_ref[k:k + 1, cols] * _shift_rows(ub, prev, FFN_CONV - 1 - k)
            outs.append(out)
        return jnp.concatenate(outs, axis=0)

    for j in range(d_ff // FF_TILE):
        gate = conv(slice(j * FF_TILE, (j + 1) * FF_TILE))
        yield
        up = conv(slice(d_ff + j * FF_TILE, d_ff + (j + 1) * FF_TILE))
        act_ref[:, j * FF_TILE:(j + 1) * FF_TILE] = (_silu(gate) * up).astype(BF16)
        yield

    x2 = x1 + _rms_norm(_dot(act_ref[...], wdown_ref[...]), n_post_ref[...])
    yield
    p_bf = p_ref[...].reshape(rows, p_ref.shape[-1]).astype(BF16)
    ple = _dot(p_bf, wple_ref[...]) * _sigmoid(_dot(x2.astype(BF16), wgate_ref[...]))
    out_ref[...] = (x2 + _rms_norm(ple, n_ple_ref[...])).reshape(n_batch, tb, d_model)


def _mixer_post_kernel(hq_ref, g_ref, hi_ref, hg_ref, gq_ref, gk_ref, gv_ref, gz_ref, ab_ref, hn_ref, gn_ref,
                       x_ref, p_ref, wout_ref, wup_ref, cw_ref, wdown_ref, wple_ref, wgate_ref,
                       n_mix_ref, n_pre_ref, n_post_ref, n_ple_ref, out_ref,
                       hst_ref, gst_ref, mix_ref, halo_ref, act_ref):
    i = pl.program_id(0)

    @pl.when(i == 0)
    def _():
        hst_ref[...] = jnp.zeros(hst_ref.shape, F32)
        gst_ref[...] = jnp.zeros(gst_ref.shape, F32)
        mix_ref[...] = jnp.zeros(mix_ref.shape, BF16)

    @pl.when(i <= 1)
    def _():
        halo_ref[...] = jnp.zeros(halo_ref.shape, F32)

    post = _post_steps(x_ref, p_ref, mix_ref, wout_ref, wup_ref, cw_ref, wdown_ref, wple_ref, wgate_ref,
                       n_mix_ref, n_pre_ref, n_post_ref, n_ple_ref, out_ref, halo_ref, act_ref)
    next(post)
    mixer = _mixer_block_steps((hq_ref, g_ref, hi_ref, hg_ref, gq_ref, gk_ref, gv_ref, gz_ref, ab_ref),
                               hn_ref, gn_ref, hst_ref, gst_ref, mix_ref)
    _run_weighted(mixer, post, MIXER_STEPS_PER_POST_STEP)


def _mixer_post(mixer_inputs, hgrn_norm, gdn_norm, x2d, p2d, post_consts, batch, seq_len, tb):
    n_tok, d_model = x2d.shape
    d_ff = post_consts[3].shape[0]
    assert n_tok == batch * seq_len and seq_len % tb == 0 and tb % CHUNK == 0 and d_ff % FF_TILE == 0
    n_blocks = seq_len // tb
    per_batch = lambda a: a.reshape(batch, seq_len, a.shape[-1])
    cur_spec = lambda width: pl.BlockSpec((batch, tb, width), lambda i: (0, jnp.minimum(i, n_blocks - 1), 0))
    prev_spec = lambda width: pl.BlockSpec((batch, tb, width), lambda i: (0, jnp.maximum(i - 1, 0), 0))
    state = pltpu.VMEM((batch, N_HEADS, HEAD_DIM, HEAD_DIM), F32)
    out = pl.pallas_call(
        _mixer_post_kernel,
        grid=(n_blocks + 1,),
        in_specs=[cur_spec(D_GROUP)] * 8 + [cur_spec(HEAD_DIM), _const_spec(hgrn_norm.shape),
                                            _const_spec(gdn_norm.shape), prev_spec(d_model),
                                            prev_spec(p2d.shape[1])]
                 + [_const_spec(a.shape) for a in post_consts],
        out_specs=prev_spec(d_model),
        out_shape=jax.ShapeDtypeStruct((batch, seq_len, d_model), F32),
        scratch_shapes=[state, state,
                        pltpu.VMEM((batch, tb, 2 * D_GROUP), BF16),
                        pltpu.VMEM((batch, HALO, 2 * d_ff), F32),
                        pltpu.VMEM((batch * tb, d_ff), BF16)],
        compiler_params=pltpu.CompilerParams(dimension_semantics=("arbitrary",),
                                             vmem_limit_bytes=V7X_VMEM_LIMIT_BYTES),
        name="mixer_post",
    )(*(per_batch(a) for a in mixer_inputs), hgrn_norm, gdn_norm, per_batch(x2d), per_batch(p2d), *post_consts)
    return out.reshape(n_tok, d_model)


def _layer(x2d, p2d, mix_norm_pre, mix_norm_post, w_in, lb_logits, hgrn_out_norm, gdn_conv, gdn_a_log,
           gdn_dt_bias, gdn_out_norm, w_out, ffn_norm_pre, ffn_norm_post, w_up, ffn_conv, w_down, w_ple,
           w_ple_gate, ple_norm_post, batch, seq_len, in_tile, mix_tile):
    n_main = 8 * D_GROUP
    n_ab = 2 * N_HEADS
    assert w_in.shape[1] == n_main + n_ab
    row = lambda v: v.reshape(1, -1).astype(F32)
    pad_lanes = lambda v: jnp.pad(row(v), ((0, 0), (0, HEAD_DIM - v.shape[-1])))
    w_main = w_in[:, :n_main].astype(BF16)
    w_ab = jnp.pad(w_in[:, n_main:], ((0, 0), (0, HEAD_DIM - n_ab))).astype(BF16)
    hq, g, hi, hg, gq, gk, gv, gz, ab = _in_proj(
        x2d, row(mix_norm_pre), w_main, w_ab, lb_logits.astype(F32), gdn_conv.astype(F32),
        pad_lanes(gdn_a_log), pad_lanes(gdn_dt_bias), seq_len, in_tile)
    post_consts = (w_out.astype(BF16), w_up.astype(BF16), ffn_conv.astype(F32), w_down.astype(BF16),
                   w_ple.astype(BF16), w_ple_gate.astype(BF16), row(mix_norm_post), row(ffn_norm_pre),
                   row(ffn_norm_post), row(ple_norm_post))
    return _mixer_post((hq, g, hi, hg, gq, gk, gv, gz, ab), row(hgrn_out_norm), row(gdn_out_norm),
                       x2d, p2d, post_consts, batch, seq_len, mix_tile)


def kernel(x, p, mix_norm_pre, mix_norm_post, w_in, hgrn_lb_logits, hgrn_out_norm, gdn_conv, gdn_a_log,
           gdn_dt_bias, gdn_out_norm, w_out, ffn_norm_pre, ffn_norm_post, w_up, ffn_conv, w_down, w_ple,
           w_ple_gate, ple_norm_post):
    batch, seq_len, d_model = x.shape
    depth = p.shape[0]
    assert depth == 1 and hgrn_lb_logits.shape[0] == 2
    x2d = x.reshape(batch * seq_len, d_model)
    for i in range(depth):
        x2d = _layer(x2d, p[i].reshape(batch * seq_len, -1), mix_norm_pre[i], mix_norm_post[i], w_in[i],
                     hgrn_lb_logits, hgrn_out_norm[i], gdn_conv[i], gdn_a_log[i], gdn_dt_bias[i],
                     gdn_out_norm[i], w_out[i], ffn_norm_pre[i], ffn_norm_post[i], w_up[i], ffn_conv[i],
                     w_down[i], w_ple[i], w_ple_gate[i], ple_norm_post[i],
                     batch, seq_len, IN_TILE, MIX_TILE)
    return x2d.reshape(batch, seq_len, d_model)
```
